```python
import math
import jax
import jax.numpy as jnp
from jax import lax
import numpy as np

D_MODEL = 1024
BATCH = 2
SEQ = 8192
DEPTH = 4
DEC_BATCH = 32
DEC_SEQ = 4
PAST_LEN = 8192
PAGE_SIZE = 128

N_HEADS = 8
HEAD_DIM = 64
D_ATT = N_HEADS * HEAD_DIM
D_RNN = D_MODEL
N_RNN_BLOCKS = 8
RNN_BLOCK = D_RNN // N_RNN_BLOCKS
CONV_W = 4
LRU_C = 8.0
MOBA_BLOCK = 256
MOBA_TOPK = 3
Q_BLOCK = 128
RMS_EPS = 1e-6
SPLITS = (D_ATT, D_ATT, D_ATT, D_ATT, D_RNN, D_RNN, D_MODEL, D_MODEL)
N_IN = sum(SPLITS)

kernel_name = "hybrid_moba_rglru_gated_decoder_step"


def _alibi_slopes():
    return jnp.asarray([2.0 ** (-8.0 * (h + 1) / N_HEADS) for h in range(N_HEADS)], jnp.float32)


def _rmsnorm(x, w):
    x32 = x.astype(jnp.float32)
    y = x32 * lax.rsqrt(jnp.mean(x32 * x32, axis=-1, keepdims=True) + RMS_EPS)
    return (y * w.astype(jnp.float32)).astype(x.dtype)


def _project(x, norm_w_l, w_in_l, qn_l, kn_l):
    h = _rmsnorm(x, norm_w_l)
    z = h @ w_in_l
    cuts = np.cumsum(SPLITS)[:-1].tolist()
    q, k, v, g_att, xr, g_rnn, m_att, m_rnn = jnp.split(z, cuts, axis=-1)
    shp = x.shape[:-1] + (N_HEADS, HEAD_DIM)
    q = _rmsnorm(q.reshape(shp), qn_l) * (HEAD_DIM ** -0.5)
    k = _rmsnorm(k.reshape(shp), kn_l)
    v = v.reshape(shp)
    return q, k, v, g_att, xr, g_rnn, m_att, m_rnn


def _moba_attend(q, q_pos, k_blk, v_blk, k_mean, own_k, own_v, own_pos, slopes, n_sel):
    B = q.shape[0]
    slope = slopes[None, None, :, None]
    dist_own = (q_pos[:, None] - own_pos[None, :]).astype(jnp.float32)[None, :, None, :]
    s_own = jnp.einsum('bthd,bshd->bths', q, own_k).astype(jnp.float32) - slope * dist_own
    causal = (own_pos[None, :] <= q_pos[:, None])[None, :, None, :]
    parts = [jnp.where(causal, s_own, -jnp.inf)]
    b_ix = jnp.arange(B)[:, None, None]
    h_ix = jnp.arange(N_HEADS)[None, None, :]
    idx_list = []
    if n_sel > 0:
        n_blocks = k_mean.shape[1]
        c_q = q_pos // MOBA_BLOCK
        gate = jnp.einsum('bthd,bnhd->bthn', q, k_mean).astype(jnp.float32)
        fully_past = (jnp.arange(n_blocks)[None, :] < c_q[:, None])[None, :, None, :]
        gate = jnp.where(fully_past, gate, -jnp.inf)
        _, idx = lax.top_k(gate, n_sel)
        valid = idx < c_q[None, :, None, None]
        offs = jnp.arange(MOBA_BLOCK, dtype=jnp.int32)
        for j in range(n_sel):
            idx_j = idx[..., j]
            k_j = k_blk[b_ix, idx_j, :, h_ix, :]
            k_pos = idx_j[..., None] * MOBA_BLOCK + offs
            dist = (q_pos[None, :, None, None] - k_pos).astype(jnp.float32)
            s_j = jnp.einsum('bthd,bthsd->bths', q, k_j).astype(jnp.float32) - slope * dist
            parts.append(jnp.where(valid[..., j, None], s_j, -jnp.inf))
            idx_list.append(idx_j)
    p = jax.nn.softmax(jnp.concatenate(parts, axis=-1), axis=-1).astype(own_v.dtype)
    n_own = own_k.shape[1]
    out = jnp.einsum('bths,bshd->bthd', p[..., :n_own], own_v)
    for j, idx_j in enumerate(idx_list):
        v_j = v_blk[b_ix, idx_j, :, h_ix, :]
        p_j = p[..., n_own + j * MOBA_BLOCK: n_own + (j + 1) * MOBA_BLOCK]
        out = out + jnp.einsum('bths,bthsd->bthd', p_j, v_j)
    return out


def _moba_prompt(q, k, v, slopes):
    B, S = q.shape[0], q.shape[1]
    nb = -(-S // MOBA_BLOCK)
    pad = nb * MOBA_BLOCK - S
    k_pad = jnp.pad(k, ((0, 0), (0, pad), (0, 0), (0, 0)))
    v_pad = jnp.pad(v, ((0, 0), (0, pad), (0, 0), (0, 0)))
    k_blk = k_pad.reshape(B, nb, MOBA_BLOCK, N_HEADS, HEAD_DIM)
    v_blk = v_pad.reshape(B, nb, MOBA_BLOCK, N_HEADS, HEAD_DIM)
    k_mean = jnp.mean(k_blk, axis=2)
    n_sel = min(MOBA_TOPK, nb - 1)

    def one_query_block(qb):
        start = qb * Q_BLOCK
        q_b = lax.dynamic_slice_in_dim(q, start, Q_BLOCK, axis=1)
        own_start = (start // MOBA_BLOCK) * MOBA_BLOCK
        own_k = lax.dynamic_slice_in_dim(k_pad, own_start, MOBA_BLOCK, axis=1)
        own_v = lax.dynamic_slice_in_dim(v_pad, own_start, MOBA_BLOCK, axis=1)
        q_pos = start + jnp.arange(Q_BLOCK, dtype=jnp.int32)
        own_pos = own_start + jnp.arange(MOBA_BLOCK, dtype=jnp.int32)
        return _moba_attend(q_b, q_pos, k_blk, v_blk, k_mean, own_k, own_v, own_pos, slopes, n_sel)

    out = lax.map(one_query_block, jnp.arange(S // Q_BLOCK, dtype=jnp.int32))
    return jnp.moveaxis(out, 0, 1).reshape(B, S, N_HEADS, HEAD_DIM)


def _moba_sample(q, k_new, v_new, k_cache_l, v_cache_l, page_table, slopes):
    DB, T = q.shape[0], q.shape[1]
    k_past = k_cache_l[page_table].reshape(DB, PAST_LEN, N_HEADS, HEAD_DIM)
    v_past = v_cache_l[page_table].reshape(DB, PAST_LEN, N_HEADS, HEAD_DIM)
    c = PAST_LEN // MOBA_BLOCK
    n_sel = min(MOBA_TOPK, c)
    k_blk = k_past[:, :c * MOBA_BLOCK].reshape(DB, c, MOBA_BLOCK, N_HEADS, HEAD_DIM)
    v_blk = v_past[:, :c * MOBA_BLOCK].reshape(DB, c, MOBA_BLOCK, N_HEADS, HEAD_DIM)
    k_mean = jnp.mean(k_blk, axis=2) if n_sel > 0 else None
    own_k = jnp.concatenate([k_past[:, c * MOBA_BLOCK:], k_new], axis=1)
    own_v = jnp.concatenate([v_past[:, c * MOBA_BLOCK:], v_new], axis=1)
    q_pos = PAST_LEN + jnp.arange(T, dtype=jnp.int32)
    own_pos = c * MOBA_BLOCK + jnp.arange(own_k.shape[1], dtype=jnp.int32)
    return _moba_attend(q, q_pos, k_blk, v_blk, k_mean, own_k, own_v, own_pos, slopes, n_sel)


def _block_diag(x, w, b):
    xb = x.reshape(x.shape[:-1] + (N_RNN_BLOCKS, RNN_BLOCK))
    return jnp.einsum('btni,nij->btnj', xb, w).reshape(x.shape) + b


def _scan_combine(c1, c2):
    a1, b1 = c1
    a2, b2 = c2
    return a1 * a2, a2 * b1 + b2


def _rglru(x, h0, w_a, b_a, w_x, b_x, lam):
    f32 = jnp.float32
    x32 = x.astype(f32)
    r = jax.nn.sigmoid(_block_diag(x32, w_a.astype(f32), b_a.astype(f32)))
    i = jax.nn.sigmoid(_block_diag(x32, w_x.astype(f32), b_x.astype(f32)))
    log_a = -LRU_C * r * jax.nn.softplus(-lam.astype(f32))
    a = jnp.exp(log_a)
    b = jnp.sqrt(-jnp.expm1(2.0 * log_a)) * (i * x32)
    b = b.at[:, 0].add(a[:, 0] * h0.astype(f32))
    _, h = lax.associative_scan(_scan_combine, (a, b), axis=1)
    return h.astype(x.dtype), h[:, -1].astype(x.dtype)


def _rnn_branch(xr, conv_prev, h0, conv_w_l, conv_b_l, w_a, b_a, w_x, b_x, lam):
    T = xr.shape[1]
    xp = jnp.concatenate([conv_prev.astype(xr.dtype), xr], axis=1)
    xc = conv_b_l + sum(xp[:, j:j + T] * conv_w_l[j] for j in range(CONV_W))
    y, h_last = _rglru(xc, h0, w_a, b_a, w_x, b_x, lam)
    return y, xp[:, T:], h_last


def _merge(att, g_att, rnn, g_rnn, m_att, m_rnn, w_ao, w_ro, w_o):
    att = att.reshape(att.shape[:-2] + (D_ATT,))
    y_a = (att * jax.nn.silu(g_att)) @ w_ao
    y_r = (rnn * jax.nn.silu(g_rnn)) @ w_ro
    return (jax.nn.sigmoid(m_att) * y_a + jax.nn.sigmoid(m_rnn) * y_r) @ w_o


def _normal(k, shape, scale):
    return scale * jax.random.normal(k, shape, jnp.float32)


def setup_inputs(seed: int = 0) -> dict:
    key = jax.random.key(seed)
    ks = jax.random.split(key, 24)
    n_pages = PAST_LEN // PAGE_SIZE
    n_used = DEC_BATCH * n_pages
    n_pool = n_used + max(1, n_used // 4)
    x_prompt = _normal(ks[0], (BATCH, SEQ, D_MODEL), 1.0)
    x_sample = _normal(ks[1], (DEC_BATCH, DEC_SEQ, D_MODEL), 1.0)
    cache_k = _normal(ks[2], (DEPTH, n_pool, PAGE_SIZE, N_HEADS, HEAD_DIM), 1.0)
    cache_v = _normal(ks[3], (DEPTH, n_pool, PAGE_SIZE, N_HEADS, HEAD_DIM), 1.0)
    state_conv = _normal(ks[4], (DEPTH, DEC_BATCH, CONV_W - 1, D_RNN), 1.0)
    state_h = _normal(ks[5], (DEPTH, DEC_BATCH, D_RNN), 0.5)
    page_table = jax.random.permutation(ks[6], n_pool)[:n_used].reshape(DEC_BATCH, n_pages).astype(jnp.int32)
    norm_w = 1.0 + _normal(ks[7], (DEPTH, D_MODEL), 0.02)
    w_in = _normal(ks[8], (DEPTH, D_MODEL, N_IN), D_MODEL ** -0.5)
    q_norm_w = 1.0 + _normal(ks[9], (DEPTH, HEAD_DIM), 0.02)
    k_norm_w = 1.0 + _normal(ks[10], (DEPTH, HEAD_DIM), 0.02)
    conv_w = _normal(ks[11], (DEPTH, CONV_W, D_RNN), CONV_W ** -0.5)
    conv_b = _normal(ks[12], (DEPTH, D_RNN), 0.02)
    w_gate_a = _normal(ks[13], (DEPTH, N_RNN_BLOCKS, RNN_BLOCK, RNN_BLOCK), RNN_BLOCK ** -0.5)
    b_gate_a = _normal(ks[14], (DEPTH, D_RNN), 0.02)
    w_gate_x = _normal(ks[15], (DEPTH, N_RNN_BLOCKS, RNN_BLOCK, RNN_BLOCK), RNN_BLOCK ** -0.5)
    b_gate_x = _normal(ks[16], (DEPTH, D_RNN), 0.02)
    u = jax.random.uniform(ks[17], (DEPTH, D_RNN), jnp.float32, minval=0.9, maxval=0.999)
    lru_lambda = jnp.log(u) - jnp.log1p(-u)
    w_attn_out = _normal(ks[18], (DEPTH, D_ATT, D_MODEL), D_ATT ** -0.5)
    w_rnn_out = _normal(ks[19], (DEPTH, D_RNN, D_MODEL), D_RNN ** -0.5)
    w_out = _normal(ks[20], (DEPTH, D_MODEL, D_MODEL), D_MODEL ** -0.5)
    return {"x_prompt": x_prompt, "x_sample": x_sample, "cache_k": cache_k, "cache_v": cache_v,
            "state_conv": state_conv, "state_h": state_h, "page_table": page_table,
            "norm_w": norm_w, "w_in": w_in, "q_norm_w": q_norm_w, "k_norm_w": k_norm_w,
            "conv_w": conv_w, "conv_b": conv_b, "w_gate_a": w_gate_a, "b_gate_a": b_gate_a,
            "w_gate_x": w_gate_x, "b_gate_x": b_gate_x, "lru_lambda": lru_lambda,
            "w_attn_out": w_attn_out, "w_rnn_out": w_rnn_out, "w_out": w_out}


def reference(x_prompt, x_sample, cache_k, cache_v, state_conv, state_h, page_table,
              norm_w, w_in, q_norm_w, k_norm_w, conv_w, conv_b, w_gate_a, b_gate_a,
              w_gate_x, b_gate_x, lru_lambda, w_attn_out, w_rnn_out, w_out):
    slopes = _alibi_slopes()
    y_p, y_s = x_prompt, x_sample
    kp, vp, cp, hp, ks_, vs_, cs_, hs_ = [], [], [], [], [], [], [], []
    for l in range(DEPTH):
        lw = (conv_w[l], conv_b[l], w_gate_a[l], b_gate_a[l], w_gate_x[l], b_gate_x[l], lru_lambda[l])
        ow = (w_attn_out[l], w_rnn_out[l], w_out[l])
        q, k, v, g_att, xr, g_rnn, m_att, m_rnn = _project(y_p, norm_w[l], w_in[l], q_norm_w[l], k_norm_w[l])
        att = _moba_prompt(q, k, v, slopes)
        conv0 = jnp.zeros((y_p.shape[0], CONV_W - 1, D_RNN), y_p.dtype)
        h0 = jnp.zeros((y_p.shape[0], D_RNN), y_p.dtype)
        rnn, c_new, h_new = _rnn_branch(xr, conv0, h0, *lw)
        y_p = y_p + _merge(att, g_att, rnn, g_rnn, m_att, m_rnn, *ow)
        kp.append(k)
        vp.append(v)
        cp.append(c_new)
        hp.append(h_new)
        q, k, v, g_att, xr, g_rnn, m_att, m_rnn = _project(y_s, norm_w[l], w_in[l], q_norm_w[l], k_norm_w[l])
        att = _moba_sample(q, k, v, cache_k[l], cache_v[l], page_table, slopes)
        rnn, c_new, h_new = _rnn_branch(xr, state_conv[l], state_h[l], *lw)
        y_s = y_s + _merge(att, g_att, rnn, g_rnn, m_att, m_rnn, *ow)
        ks_.append(k)
        vs_.append(v)
        cs_.append(c_new)
        hs_.append(h_new)
    k_prompt = jnp.stack(kp)
    v_prompt = jnp.stack(vp)
    conv_prompt = jnp.stack(cp)
    h_prompt = jnp.stack(hp)
    k_sample = jnp.stack(ks_)
    v_sample = jnp.stack(vs_)
    conv_sample = jnp.stack(cs_)
    h_sample = jnp.stack(hs_)
    return (y_p, y_s, k_prompt, v_prompt, conv_prompt, h_prompt, k_sample, v_sample, conv_sample, h_sample)
```

```python
import functools

import numpy as np
import jax
import jax.numpy as jnp
from jax import lax
from jax.experimental import pallas as pl
from jax.experimental.pallas import tpu as pltpu

F32 = jnp.float32
BF16 = jnp.bfloat16

D_MODEL = 1024
N_HEADS = 8
HEAD_DIM = 64
D_ATT = N_HEADS * HEAD_DIM
D_RNN = D_MODEL
N_RNN_BLOCKS = 8
RNN_BLOCK = D_RNN // N_RNN_BLOCKS
CONV_W = 4
LRU_C = 8.0
MOBA_BLOCK = 256
MOBA_TOPK = 3
PAGE_SIZE = 128
RMS_EPS = 1e-6
SPLITS = (D_ATT, D_ATT, D_ATT, D_ATT, D_RNN, D_RNN, D_MODEL, D_MODEL)
N_IN = sum(SPLITS)
CUTS = tuple(int(c) for c in np.cumsum((0,) + SPLITS))

LANES = 128
HEADS_PER_SLAB = LANES // HEAD_DIM
N_SLABS = D_ATT // LANES
MASK_BIAS = -1e30
VMEM_LIMIT = 56 * 1024 * 1024

PAGES_PER_STEP = 8
BLOCKS_PER_STEP = PAGES_PER_STEP * PAGE_SIZE // MOBA_BLOCK
PAGES_PER_BLOCK = MOBA_BLOCK // PAGE_SIZE
SAMPLE_ROWS = LANES


def _hilo(x):
    hi = x.astype(BF16)
    lo = (x - hi.astype(F32)).astype(BF16)
    return hi, lo


def _dot_nt(a, b):
    return lax.dot_general(a, b, (((1,), (1,)), ((), ())), preferred_element_type=F32)


def _dot_nt_precise(a, b):
    ah, al = _hilo(a)
    bh, bl = _hilo(b)
    return _dot_nt(ah, bh) + _dot_nt(ah, bl) + _dot_nt(al, bh)


def _top_blocks(gate, col, n_sel):
    sel = jnp.zeros(gate.shape, jnp.bool_)
    colf = col.astype(F32)
    for _ in range(n_sel):
        m = jnp.max(gate, axis=-1, keepdims=True)
        idx = jnp.min(jnp.where(gate == m, colf, float(gate.shape[-1])), axis=-1, keepdims=True)
        hit = colf == idx
        sel = jnp.logical_or(sel, jnp.logical_and(hit, m > -jnp.inf))
        gate = jnp.where(hit, -jnp.inf, gate)
    return sel


def _proj_kernel(prompt, x_ref, nw_ref, w_ref, qn_ref, kn_ref, gmean_ref, *out_refs):
    if prompt:
        q_ref, k_ref, v_ref, kb_ref, vb_ref, ga_ref, xr_ref, gr_ref, ma_ref, mr_ref, km_ref = out_refs
    else:
        q_ref, k_ref, v_ref, ga_ref, xr_ref, gr_ref, ma_ref, mr_ref = out_refs
    x = x_ref[...]
    ms = jnp.mean(x * x, axis=-1, keepdims=True)
    h = (x * lax.rsqrt(ms + RMS_EPS) * nw_ref[...]).astype(BF16)

    def part(i):
        return jnp.dot(h, w_ref[:, CUTS[i]:CUTS[i + 1]], preferred_element_type=F32)

    def head_norm(z, w):
        sq_hi, sq_lo = _hilo(z * z)
        g = gmean_ref[...]
        ms_h = jnp.dot(sq_hi, g, preferred_element_type=F32) + jnp.dot(sq_lo, g, preferred_element_type=F32)
        return z * lax.rsqrt(ms_h + RMS_EPS) * w

    q_ref[...] = head_norm(part(0), qn_ref[...]) * (HEAD_DIM ** -0.5)
    k = head_norm(part(1), kn_ref[...])
    v = part(2)
    k_ref[...] = k
    v_ref[...] = v
    ga_ref[...] = part(3)
    xr_ref[...] = part(4)
    gr_ref[...] = part(5)
    ma_ref[...] = part(6)
    mr_ref[...] = part(7)
    if prompt:
        kb_ref[...] = k.astype(BF16)
        vb_ref[...] = v.astype(BF16)
        km_ref[...] = jnp.mean(k, axis=0, keepdims=True)


def _proj(x, norm_w, w_in_b, qn_t, kn_t, gmean, *, prompt):
    rows = x.shape[0]
    tm = MOBA_BLOCK if prompt else rows
    assert rows % tm == 0
    n_tiles = rows // tm

    def rowspec(width):
        return pl.BlockSpec((tm, width), lambda i: (i, 0))

    def const(shape):
        return pl.BlockSpec(shape, lambda i: (0,) * len(shape))

    widths = (D_ATT, D_ATT, D_ATT) + ((D_ATT, D_ATT) if prompt else ()) + (D_ATT, D_RNN, D_RNN, D_MODEL, D_MODEL)
    dtypes = (F32, F32, F32) + ((BF16, BF16) if prompt else ()) + (F32,) * 5
    out_shape = [jax.ShapeDtypeStruct((rows, w), d) for w, d in zip(widths, dtypes)]
    out_specs = [rowspec(w) for w in widths]
    if prompt:
        out_shape.append(jax.ShapeDtypeStruct((n_tiles, 1, D_ATT), F32))
        out_specs.append(pl.BlockSpec((None, 1, D_ATT), lambda i: (i, 0, 0)))
    return pl.pallas_call(
        functools.partial(_proj_kernel, prompt),
        grid=(n_tiles,),
        in_specs=[rowspec(D_MODEL), const((1, D_MODEL)), const((D_MODEL, N_IN)),
                  const((1, D_ATT)), const((1, D_ATT)), const((D_ATT, D_ATT))],
        out_specs=out_specs,
        out_shape=out_shape,
        compiler_params=pltpu.CompilerParams(dimension_semantics=("arbitrary",), vmem_limit_bytes=VMEM_LIMIT),
        name="proj_prompt" if prompt else "proj_sample",
    )(x, norm_w, w_in_b, qn_t, kn_t, gmean)


def _moba_prompt_kernel(slopes_ref, q_ref, kb_ref, vb_ref, km_ref, o_ref):
    slab = pl.program_id(1)
    qb = pl.program_id(2)
    n_blocks = km_ref.shape[0]
    tq = MOBA_BLOCK
    row = lax.broadcasted_iota(jnp.int32, (tq, tq), 0)
    colk = lax.broadcasted_iota(jnp.int32, (tq, tq), 1)
    rel = (row - colk).astype(F32)
    causal = colk <= row
    colb = lax.broadcasted_iota(jnp.int32, (tq, n_blocks), 1)
    lane = lax.broadcasted_iota(jnp.int32, (tq, LANES), 1)
    q = q_ref[...]
    km = km_ref[...]
    start = pl.multiple_of(qb * tq, tq)
    k_own = kb_ref[pl.ds(start, tq), :]
    v_own = vb_ref[pl.ds(start, tq), :]

    q_h, bias_h, slope_h, state = [], [], [], []
    for hh in range(HEADS_PER_SLAB):
        slope = slopes_ref[slab * HEADS_PER_SLAB + hh]
        in_head = (lane >= hh * HEAD_DIM) & (lane < (hh + 1) * HEAD_DIM)
        qm = jnp.where(in_head, q, 0.0)
        gate = _dot_nt_precise(qm, km)
        gate = jnp.where(colb < qb, gate, -jnp.inf)
        sel = _top_blocks(gate, colb, min(MOBA_TOPK, n_blocks - 1))
        bias = jnp.where(sel, 0.0, MASK_BIAS)
        qmb = qm.astype(BF16)
        s = _dot_nt(qmb, k_own) - slope * rel
        s = jnp.where(causal, s, -jnp.inf)
        m = jnp.max(s, axis=-1, keepdims=True)
        p = jnp.exp(s - m)
        l = jnp.sum(p, axis=-1, keepdims=True)
        acc = jnp.dot(p.astype(BF16), v_own, preferred_element_type=F32)
        q_h.append(qmb)
        bias_h.append(bias)
        slope_h.append(slope)
        state += [m, l, acc]

    def body(n, carry):
        off = pl.multiple_of(n * tq, tq)
        k_n = kb_ref[pl.ds(off, tq), :]
        v_n = vb_ref[pl.ds(off, tq), :]
        block_gap = ((qb - n) * tq).astype(F32)
        new = []
        for hh in range(HEADS_PER_SLAB):
            m, l, acc = carry[3 * hh:3 * hh + 3]
            slope = slope_h[hh]
            bias_n = jnp.sum(jnp.where(colb == n, bias_h[hh], 0.0), axis=-1, keepdims=True)
            s = _dot_nt(q_h[hh], k_n) - slope * rel + (bias_n - slope * block_gap)
            m_new = jnp.maximum(m, jnp.max(s, axis=-1, keepdims=True))
            alpha = jnp.exp(m - m_new)
            p = jnp.exp(s - m_new)
            l = alpha * l + jnp.sum(p, axis=-1, keepdims=True)
            acc = alpha * acc + jnp.dot(p.astype(BF16), v_n, preferred_element_type=F32)
            new += [m_new, l, acc]
        return tuple(new)

    state = lax.fori_loop(0, qb, body, tuple(state))
    out = state[2] / state[1]
    for hh in range(1, HEADS_PER_SLAB):
        out = jnp.where(lane >= hh * HEAD_DIM, state[3 * hh + 2] / state[3 * hh + 1], out)
    o_ref[...] = out


def _moba_prompt(slopes, q, kb, vb, kmean):
    batch, seq, _ = q.shape
    n_blocks = seq // MOBA_BLOCK
    assert seq % MOBA_BLOCK == 0 and n_blocks > MOBA_TOPK
    return pl.pallas_call(
        _moba_prompt_kernel,
        grid=(batch, N_SLABS, n_blocks),
        in_specs=[pl.BlockSpec(memory_space=pltpu.SMEM),
                  pl.BlockSpec((None, MOBA_BLOCK, LANES), lambda b, s, i: (b, i, s)),
                  pl.BlockSpec((None, seq, LANES), lambda b, s, i: (b, 0, s)),
                  pl.BlockSpec((None, seq, LANES), lambda b, s, i: (b, 0, s)),
                  pl.BlockSpec((None, n_blocks, LANES), lambda b, s, i: (b, 0, s))],
        out_specs=pl.BlockSpec((None, MOBA_BLOCK, LANES), lambda b, s, i: (b, i, s)),
        out_shape=jax.ShapeDtypeStruct((batch, seq, D_ATT), F32),
        compiler_params=pltpu.CompilerParams(dimension_semantics=("arbitrary",) * 3, vmem_limit_bytes=VMEM_LIMIT),
        name="moba_prompt",
    )(slopes, q, kb, vb, kmean)


def _moba_sample_kernel(n_past_blocks, past_len, pt_ref, q_ref, kn_ref, vnt_ref, slope_ref, tpos_ref, hmask_ref,
                        *rest):
    k_pages = rest[:PAGES_PER_STEP]
    v_pages = rest[PAGES_PER_STEP:2 * PAGES_PER_STEP]
    o_ref, o_all, m_all, l_all, g_all = rest[2 * PAGES_PER_STEP:]
    del pt_ref
    j = pl.program_id(1)
    n_tok = q_ref.shape[0]
    n_rows = n_tok * N_HEADS
    hmask = hmask_ref[...]
    slope = slope_ref[...]
    tpos = tpos_ref[...]
    q = q_ref[...]
    qbd = jnp.concatenate([jnp.broadcast_to(q[t:t + 1, :], (N_HEADS, D_ATT)) for t in range(n_tok)]
                          + [jnp.zeros((SAMPLE_ROWS - n_rows, D_ATT), F32)], axis=0) * hmask
    q_hi, q_lo = _hilo(qbd)
    colk = lax.broadcasted_iota(jnp.int32, (SAMPLE_ROWS, MOBA_BLOCK), 1).astype(F32)
    colb = lax.broadcasted_iota(jnp.int32, (SAMPLE_ROWS, LANES), 1)

    @pl.when(j == 0)
    def _():
        m_all[...] = jnp.full(m_all.shape, -jnp.inf, F32)
        l_all[...] = jnp.zeros(l_all.shape, F32)
        g_all[...] = jnp.full(g_all.shape, -jnp.inf, F32)

    for i in range(BLOCKS_PER_STEP):
        n = j * BLOCKS_PER_STEP + i
        pages = range(i * PAGES_PER_BLOCK, (i + 1) * PAGES_PER_BLOCK)
        kt = jnp.concatenate([k_pages[p][...] for p in pages], axis=1).astype(BF16)
        vt = jnp.concatenate([v_pages[p][...] for p in pages], axis=1).astype(BF16)
        raw = jnp.dot(q_hi, kt, preferred_element_type=F32) + jnp.dot(q_lo, kt, preferred_element_type=F32)
        gate = jnp.mean(raw, axis=-1, keepdims=True)
        dist = (past_len + tpos - (n * MOBA_BLOCK).astype(F32)) - colk
        s = raw - slope * dist
        m = jnp.max(s, axis=-1, keepdims=True)
        p = jnp.exp(s - m)
        l = jnp.sum(p, axis=-1, keepdims=True)
        o_all[n] = _dot_nt(vt, p.astype(BF16))
        here = colb == n
        m_all[...] = jnp.where(here, m, m_all[...])
        l_all[...] = jnp.where(here, l, l_all[...])
        g_all[...] = jnp.where(here, gate, g_all[...])

    @pl.when(j == pl.num_programs(1) - 1)
    def _():
        sel = _top_blocks(g_all[...], colb, min(MOBA_TOPK, n_past_blocks))
        kn = kn_ref[...]
        s_own = []
        for t in range(n_tok):
            s_t = jnp.sum(qbd * kn[t:t + 1, :], axis=-1, keepdims=True) - slope * (tpos - float(t))
            s_own.append(jnp.where(tpos >= float(t), s_t, -jnp.inf))
        m_own = functools.reduce(jnp.maximum, s_own)
        m_sel = jnp.max(jnp.where(sel, m_all[...], -jnp.inf), axis=-1, keepdims=True)
        m_tot = jnp.maximum(m_own, m_sel)
        w = jnp.where(sel, jnp.exp(m_all[...] - m_tot), 0.0)
        l_tot = jnp.sum(w * l_all[...], axis=-1, keepdims=True)
        for t in range(n_tok):
            p_t = jnp.exp(s_own[t] - m_tot)
            l_tot = l_tot + p_t
            w = jnp.where(colb == n_past_blocks + t, p_t, w)
        wt = jnp.transpose(w / l_tot)
        acc = jnp.zeros((D_ATT, SAMPLE_ROWS), F32)
        for n in range(n_past_blocks):
            acc = acc + o_all[n] * wt[n:n + 1, :]
        vnt = vnt_ref[...]
        for t in range(n_tok):
            acc = acc + vnt[:, t:t + 1] * wt[n_past_blocks + t:n_past_blocks + t + 1, :]
        out = jnp.transpose(acc)[:n_rows] * hmask[:n_rows]
        o_ref[...] = jnp.sum(out.reshape(n_tok, N_HEADS, D_ATT), axis=1)


def _moba_sample(layer, page_table, q, k_new, v_new_t, cache_kt, cache_vt, slope_rows, tpos_rows, hmask, past_len):
    dec_batch, n_tok, _ = q.shape
    n_pages = page_table.shape[1]
    assert past_len % MOBA_BLOCK == 0 and n_pages * PAGE_SIZE == past_len and n_pages % PAGES_PER_STEP == 0
    n_past_blocks = past_len // MOBA_BLOCK
    assert SAMPLE_ROWS == LANES and n_tok * N_HEADS <= SAMPLE_ROWS and 0 < n_past_blocks <= LANES - n_tok
    n_steps = n_pages // PAGES_PER_STEP

    def tokspec():
        return pl.BlockSpec((None, n_tok, D_ATT), lambda b, j, pt: (b, 0, 0))

    def const(shape):
        return pl.BlockSpec(shape, lambda b, j, pt: (0,) * len(shape))

    def pagespec(i):
        return pl.BlockSpec((None, None, D_ATT, PAGE_SIZE),
                            lambda b, j, pt: (layer, pt[b * n_pages + j * PAGES_PER_STEP + i], 0, 0))

    grid_spec = pltpu.PrefetchScalarGridSpec(
        num_scalar_prefetch=1,
        grid=(dec_batch, n_steps),
        in_specs=[tokspec(), tokspec(), pl.BlockSpec((None, D_ATT, n_tok), lambda b, j, pt: (b, 0, 0)),
                  const((SAMPLE_ROWS, 1)), const((SAMPLE_ROWS, 1)), const((SAMPLE_ROWS, D_ATT))]
        + [pagespec(i) for i in range(PAGES_PER_STEP)] * 2,
        out_specs=tokspec(),
        scratch_shapes=[pltpu.VMEM((n_past_blocks, D_ATT, SAMPLE_ROWS), F32),
                        pltpu.VMEM((SAMPLE_ROWS, LANES), F32),
                        pltpu.VMEM((SAMPLE_ROWS, LANES), F32),
                        pltpu.VMEM((SAMPLE_ROWS, LANES), F32)],
    )
    return pl.pallas_call(
        functools.partial(_moba_sample_kernel, n_past_blocks, float(past_len)),
        grid_spec=grid_spec,
        out_shape=jax.ShapeDtypeStruct((dec_batch, n_tok, D_ATT), F32),
        compiler_params=pltpu.CompilerParams(dimension_semantics=("arbitrary",) * 2, vmem_limit_bytes=VMEM_LIMIT),
        name="moba_sample",
    )(page_table.reshape(-1), q, k_new, v_new_t, slope_rows, tpos_rows, hmask,
      *([cache_kt] * PAGES_PER_STEP), *([cache_vt] * PAGES_PER_STEP))


def _rglru_coeffs(xc, wa_ref, ba_ref, wx_ref, bx_ref, lam_ref):
    xb = xc.astype(BF16)

    def block_diag(w_ref):
        return jnp.concatenate(
            [jnp.dot(xb[:, n * RNN_BLOCK:(n + 1) * RNN_BLOCK], w_ref[n], preferred_element_type=F32)
             for n in range(N_RNN_BLOCKS)], axis=-1)

    r = jax.nn.sigmoid(block_diag(wa_ref) + ba_ref[...])
    gate_i = jax.nn.sigmoid(block_diag(wx_ref) + bx_ref[...])
    z = -lam_ref[...]
    softplus = jnp.maximum(z, 0.0) + jnp.log1p(jnp.exp(-jnp.abs(z)))
    log_a = -LRU_C * r * softplus
    a = jnp.exp(log_a)
    b = jnp.sqrt(1.0 - a * a) * (gate_i * xc)
    return a, b


def _rglru_prompt_kernel(x_ref, cw_ref, cb_ref, wa_ref, ba_ref, wx_ref, bx_ref, lam_ref,
                         y_ref, conv_ref, hlast_ref, xtail, hcarry):
    t = pl.program_id(1)
    tt = x_ref.shape[0]

    @pl.when(t == 0)
    def _():
        xtail[...] = jnp.zeros(xtail.shape, F32)
        hcarry[...] = jnp.zeros(hcarry.shape, F32)

    x = x_ref[...]
    tail = xtail[...]
    row = lax.broadcasted_iota(jnp.int32, (tt, D_RNN), 0)
    row8 = lax.broadcasted_iota(jnp.int32, (8, D_RNN), 0)
    xc = cb_ref[...] + x * cw_ref[CONV_W - 1:CONV_W, :]
    for d in range(1, CONV_W):
        head = jnp.where(row8 < d, pltpu.roll(tail, d, 0), pltpu.roll(x[:8], d, 0))
        xs = jnp.concatenate([head, pltpu.roll(x, d, 0)[8:]], axis=0)
        xc = xc + xs * cw_ref[CONV_W - 1 - d:CONV_W - d, :]
    a, b = _rglru_coeffs(xc, wa_ref, ba_ref, wx_ref, bx_ref, lam_ref)
    d = 1
    while d < tt:
        keep = row >= d
        b = jnp.where(keep, a * pltpu.roll(b, d, 0) + b, b)
        a = jnp.where(keep, a * pltpu.roll(a, d, 0), a)
        d *= 2
    h = a * hcarry[...] + b
    y_ref[...] = h
    xtail[...] = x[tt - 8:]
    hcarry[...] = h[tt - 1:tt]

    @pl.when(t == pl.num_programs(1) - 1)
    def _():
        conv_ref[...] = x_ref[tt - (CONV_W - 1):tt, :]
        hlast_ref[...] = h[tt - 1:tt]


def _rglru_prompt(xr, conv_w, conv_b, wa_b, b_a, wx_b, b_x, lam):
    batch, seq, _ = xr.shape
    tt = 256
    assert seq % tt == 0 and tt >= 8

    def const(shape):
        return pl.BlockSpec(shape, lambda b, t: (0,) * len(shape))

    wspec = const((N_RNN_BLOCKS, RNN_BLOCK, RNN_BLOCK))
    vec = const((1, D_RNN))
    return pl.pallas_call(
        _rglru_prompt_kernel,
        grid=(batch, seq // tt),
        in_specs=[pl.BlockSpec((None, tt, D_RNN), lambda b, t: (b, t, 0)),
                  const((CONV_W, D_RNN)), vec, wspec, vec, wspec, vec, vec],
        out_specs=[pl.BlockSpec((None, tt, D_RNN), lambda b, t: (b, t, 0)),
                   pl.BlockSpec((None, CONV_W - 1, D_RNN), lambda b, t: (b, 0, 0)),
                   pl.BlockSpec((None, 1, D_RNN), lambda b, t: (b, 0, 0))],
        out_shape=[jax.ShapeDtypeStruct((batch, seq, D_RNN), F32),
                   jax.ShapeDtypeStruct((batch, CONV_W - 1, D_RNN), F32),
                   jax.ShapeDtypeStruct((batch, 1, D_RNN), F32)],
        scratch_shapes=[pltpu.VMEM((8, D_RNN), F32), pltpu.VMEM((1, D_RNN), F32)],
        compiler_params=pltpu.CompilerParams(dimension_semantics=("arbitrary",) * 2, vmem_limit_bytes=VMEM_LIMIT),
        name="rglru_prompt",
    )(xr, conv_w, conv_b, wa_b, b_a, wx_b, b_x, lam)


def _rglru_sample_kernel(x_ref, prev_ref, h0_ref, cw_ref, cb_ref, wa_ref, ba_ref, wx_ref, bx_ref, lam_ref,
                         y_ref, hlast_ref):
    n_tok, n_seq = x_ref.shape[0], x_ref.shape[1]
    xp = [prev_ref[i] for i in range(CONV_W - 1)] + [x_ref[i] for i in range(n_tok)]
    xc = [cb_ref[...] + functools.reduce(lambda u, w: u + w, [xp[t + j] * cw_ref[j:j + 1, :] for j in range(CONV_W)])
          for t in range(n_tok)]
    a, b = _rglru_coeffs(jnp.concatenate(xc, axis=0), wa_ref, ba_ref, wx_ref, bx_ref, lam_ref)
    h = h0_ref[...]
    for t in range(n_tok):
        h = a[t * n_seq:(t + 1) * n_seq] * h + b[t * n_seq:(t + 1) * n_seq]
        y_ref[t] = h
    hlast_ref[...] = h


def _rglru_sample(xr_t, prev_t, h0, conv_w, conv_b, wa_b, b_a, wx_b, b_x, lam):
    n_tok, n_seq, _ = xr_t.shape
    assert n_seq % 8 == 0
    return pl.pallas_call(
        _rglru_sample_kernel,
        out_shape=[jax.ShapeDtypeStruct((n_tok, n_seq, D_RNN), F32), jax.ShapeDtypeStruct((n_seq, D_RNN), F32)],
        compiler_params=pltpu.CompilerParams(vmem_limit_bytes=VMEM_LIMIT),
        name="rglru_sample",
    )(xr_t, prev_t, h0, conv_w, conv_b, wa_b, b_a, wx_b, b_x, lam)


def _merge_kernel(x_ref, att_ref, ga_ref, rnn_ref, gr_ref, ma_ref, mr_ref, wao_ref, wro_ref, wo_ref, y_ref):
    y_a = jnp.dot((att_ref[...] * jax.nn.silu(ga_ref[...])).astype(BF16), wao_ref[...], preferred_element_type=F32)
    y_r = jnp.dot((rnn_ref[...] * jax.nn.silu(gr_ref[...])).astype(BF16), wro_ref[...], preferred_element_type=F32)
    z = jax.nn.sigmoid(ma_ref[...]) * y_a + jax.nn.sigmoid(mr_ref[...]) * y_r
    y_ref[...] = x_ref[...] + jnp.dot(z.astype(BF16), wo_ref[...], preferred_element_type=F32)


def _merge(x, att, ga, rnn, gr, ma, mr, wao_b, wro_b, wo_b):
    rows = x.shape[0]
    tm = min(rows, 256)
    assert rows % tm == 0

    def rowspec(width):
        return pl.BlockSpec((tm, width), lambda i: (i, 0))

    def const(shape):
        return pl.BlockSpec(shape, lambda i: (0,) * len(shape))

    return pl.pallas_call(
        _merge_kernel,
        grid=(rows // tm,),
        in_specs=[rowspec(D_MODEL), rowspec(D_ATT), rowspec(D_ATT), rowspec(D_RNN), rowspec(D_RNN),
                  rowspec(D_MODEL), rowspec(D_MODEL),
                  const((D_ATT, D_MODEL)), const((D_RNN, D_MODEL)), const((D_MODEL, D_MODEL))],
        out_specs=rowspec(D_MODEL),
        out_shape=jax.ShapeDtypeStruct((rows, D_MODEL), F32),
        compiler_params=pltpu.CompilerParams(dimension_semantics=("arbitrary",), vmem_limit_bytes=VMEM_LIMIT),
        name="merge",
    )(x, att, ga, rnn, gr, ma, mr, wao_b, wro_b, wo_b)


def kernel(x_prompt, x_sample, cache_k, cache_v, state_conv, state_h, page_table, norm_w, w_in, q_norm_w, k_norm_w,
           conv_w, conv_b, w_gate_a, b_gate_a, w_gate_x, b_gate_x, lru_lambda, w_attn_out, w_rnn_out, w_out):
    batch, seq, _ = x_prompt.shape
    dec_batch, dec_seq, _ = x_sample.shape
    depth = w_in.shape[0]
    n_pool = cache_k.shape[1]
    past_len = page_table.shape[1] * PAGE_SIZE

    slopes = jnp.asarray([2.0 ** (-8.0 * (h + 1) / N_HEADS) for h in range(N_HEADS)], F32)
    head_of_lane = np.arange(D_ATT) // HEAD_DIM
    gmean = jnp.asarray((head_of_lane[:, None] == head_of_lane[None, :]) / HEAD_DIM, BF16)
    n_rows = dec_seq * N_HEADS
    row_head = np.arange(SAMPLE_ROWS) % N_HEADS
    row_live = np.arange(SAMPLE_ROWS) < n_rows
    slope_rows = jnp.asarray(np.where(row_live, 2.0 ** (-8.0 * (row_head + 1) / N_HEADS), 0.0), F32)[:, None]
    tpos_rows = jnp.asarray(np.where(row_live, np.arange(SAMPLE_ROWS) // N_HEADS, 0), F32)[:, None]
    hmask = jnp.asarray((row_head[:, None] == head_of_lane[None, :]) & row_live[:, None], F32)

    w_in_b = w_in.astype(BF16)
    wa_b = w_gate_a.astype(BF16)
    wx_b = w_gate_x.astype(BF16)
    wao_b = w_attn_out.astype(BF16)
    wro_b = w_rnn_out.astype(BF16)
    wo_b = w_out.astype(BF16)
    cache_kt = jnp.transpose(cache_k, (0, 1, 3, 4, 2)).reshape(depth, n_pool, D_ATT, PAGE_SIZE)
    cache_vt = jnp.transpose(cache_v, (0, 1, 3, 4, 2)).reshape(depth, n_pool, D_ATT, PAGE_SIZE)

    y_p = x_prompt.reshape(batch * seq, D_MODEL)
    y_s = x_sample.reshape(dec_batch * dec_seq, D_MODEL)
    outs = {name: [] for name in ("kp", "vp", "cp", "hp", "ks", "vs", "cs", "hs")}
    for l in range(depth):
        nw = norm_w[l][None, :]
        qn_t = jnp.tile(q_norm_w[l], N_HEADS)[None, :]
        kn_t = jnp.tile(k_norm_w[l], N_HEADS)[None, :]
        rnn_w = (conv_w[l], conv_b[l][None, :], wa_b[l], b_gate_a[l][None, :], wx_b[l], b_gate_x[l][None, :],
                 lru_lambda[l][None, :])
        out_w = (wao_b[l], wro_b[l], wo_b[l])

        q, k, v, kb, vb, ga, xr, gr, ma, mr, kmean = _proj(y_p, nw, w_in_b[l], qn_t, kn_t, gmean, prompt=True)
        att = _moba_prompt(slopes, q.reshape(batch, seq, D_ATT), kb.reshape(batch, seq, D_ATT),
                           vb.reshape(batch, seq, D_ATT), kmean.reshape(batch, seq // MOBA_BLOCK, D_ATT))
        rnn, c_new, h_new = _rglru_prompt(xr.reshape(batch, seq, D_RNN), *rnn_w)
        y_p = _merge(y_p, att.reshape(batch * seq, D_ATT), ga, rnn.reshape(batch * seq, D_RNN), gr, ma, mr, *out_w)
        outs["kp"].append(k.reshape(batch, seq, N_HEADS, HEAD_DIM))
        outs["vp"].append(v.reshape(batch, seq, N_HEADS, HEAD_DIM))
        outs["cp"].append(c_new)
        outs["hp"].append(h_new.reshape(batch, D_RNN))

        q, k, v, ga, xr, gr, ma, mr = _proj(y_s, nw, w_in_b[l], qn_t, kn_t, gmean, prompt=False)
        att = _moba_sample(l, page_table, q.reshape(dec_batch, dec_seq, D_ATT), k.reshape(dec_batch, dec_seq, D_ATT),
                           jnp.swapaxes(v.reshape(dec_batch, dec_seq, D_ATT), 1, 2), cache_kt, cache_vt,
                           slope_rows, tpos_rows, hmask, past_len)
        xr_t = jnp.swapaxes(xr.reshape(dec_batch, dec_seq, D_RNN), 0, 1)
        prev_t = jnp.swapaxes(state_conv[l], 0, 1)
        rnn_t, h_new = _rglru_sample(xr_t, prev_t, state_h[l], *rnn_w)
        rnn = jnp.swapaxes(rnn_t, 0, 1).reshape(dec_batch * dec_seq, D_RNN)
        c_new = jnp.swapaxes(jnp.concatenate([prev_t, xr_t], axis=0)[dec_seq:], 0, 1)
        y_s = _merge(y_s, att.reshape(dec_batch * dec_seq, D_ATT), ga, rnn, gr, ma, mr, *out_w)
        outs["ks"].append(k.reshape(dec_batch, dec_seq, N_HEADS, HEAD_DIM))
        outs["vs"].append(v.reshape(dec_batch, dec_seq, N_HEADS, HEAD_DIM))
        outs["cs"].append(c_new)
        outs["hs"].append(h_new)

    return (y_p.reshape(batch, seq, D_MODEL), y_s.reshape(dec_batch, dec_seq, D_MODEL),
            jnp.stack(outs["kp"]), jnp.stack(outs["vp"]), jnp.stack(outs["cp"]), jnp.stack(outs["hp"]),
            jnp.stack(outs["ks"]), jnp.stack(outs["vs"]), jnp.stack(outs["cs"]), jnp.stack(outs["hs"]))
```

```python
import functools
import math

import numpy as np
import jax
import jax.numpy as jnp
from jax import lax
from jax.experimental import pallas as pl
from jax.experimental.pallas import tpu as pltpu

F32 = jnp.float32
BF16 = jnp.bfloat16

D_MODEL = 1024
N_HEADS = 8
HEAD_DIM = 64
D_ATT = N_HEADS * HEAD_DIM
D_RNN = D_MODEL
N_RNN_BLOCKS = 8
RNN_BLOCK = D_RNN // N_RNN_BLOCKS
CONV_W = 4
LRU_C = 8.0
MOBA_BLOCK = 256
MOBA_TOPK = 3
PAGE_SIZE = 128
RMS_EPS = 1e-6
SPLITS = (D_ATT, D_ATT, D_ATT, D_ATT, D_RNN, D_RNN, D_MODEL, D_MODEL)
N_IN = sum(SPLITS)
CUTS = tuple(int(c) for c in np.cumsum((0,) + SPLITS))

LANES = 128
HEADS_PER_SLAB = LANES // HEAD_DIM
N_SLABS = D_ATT // LANES
MASK_BIAS = -1e30
VMEM_LIMIT = 56 * 1024 * 1024

PAGES_PER_STEP = 8
BLOCKS_PER_STEP = PAGES_PER_STEP * PAGE_SIZE // MOBA_BLOCK
PAGES_PER_BLOCK = MOBA_BLOCK // PAGE_SIZE
SAMPLE_ROWS = LANES

AUG_ROW_ONES_I = 0
AUG_ROW_ONES_QB = 1
AUG_ROW_KEY = 2
AUG_ROW_BLOCK = 3
AUG_ROWS = 4
ROW_CHUNK = 64
HEADS_PER_STEP = 4
assert all(math.frexp(2.0 ** (-8.0 * (h + 1) / N_HEADS))[0] == 0.5 for h in range(N_HEADS)) and MOBA_BLOCK <= 256


def _hilo(x):
    hi = x.astype(BF16)
    lo = (x - hi.astype(F32)).astype(BF16)
    return hi, lo


def _dot_nt(a, b):
    return lax.dot_general(a, b, (((1,), (1,)), ((), ())), preferred_element_type=F32)


def _dot_nt_precise(a, b):
    ah, al = _hilo(a)
    bh, bl = _hilo(b)
    return _dot_nt(ah, bh) + _dot_nt(ah, bl) + _dot_nt(al, bh)


def _top_blocks(gate, col, n_sel):
    sel = jnp.zeros(gate.shape, jnp.bool_)
    colf = col.astype(F32)
    for _ in range(n_sel):
        m = jnp.max(gate, axis=-1, keepdims=True)
        idx = jnp.min(jnp.where(gate == m, colf, float(gate.shape[-1])), axis=-1, keepdims=True)
        hit = colf == idx
        sel = jnp.logical_or(sel, jnp.logical_and(hit, m > -jnp.inf))
        gate = jnp.where(hit, -jnp.inf, gate)
    return sel


def _proj_parts(x_ref, nw_ref, w_ref, qn_ref, kn_ref, gmean_ref):
    x = x_ref[...]
    ms = jnp.mean(x * x, axis=-1, keepdims=True)
    h = (x * lax.rsqrt(ms + RMS_EPS) * nw_ref[...]).astype(BF16)

    def part(i):
        return jnp.dot(h, w_ref[:, CUTS[i]:CUTS[i + 1]], preferred_element_type=F32)

    def head_norm(z, w):
        sq_hi, sq_lo = _hilo(z * z)
        g = gmean_ref[...]
        ms_h = jnp.dot(sq_hi, g, preferred_element_type=F32) + jnp.dot(sq_lo, g, preferred_element_type=F32)
        return z * lax.rsqrt(ms_h + RMS_EPS) * w

    q = head_norm(part(0), qn_ref[...]) * (HEAD_DIM ** -0.5)
    k = head_norm(part(1), kn_ref[...])
    return q, k, part


def _proj_sample_kernel(x_ref, nw_ref, w_ref, qn_ref, kn_ref, gmean_ref,
                        q_ref, k_ref, v_ref, ga_ref, xr_ref, gr_ref, ma_ref, mr_ref):
    q, k, part = _proj_parts(x_ref, nw_ref, w_ref, qn_ref, kn_ref, gmean_ref)
    q_ref[...] = q
    k_ref[...] = k
    for i, ref in zip(range(2, 8), (v_ref, ga_ref, xr_ref, gr_ref, ma_ref, mr_ref)):
        ref[...] = part(i)


def _proj_prompt_kernel(n_blocks, x_ref, nw_ref, w_ref, qn_ref, kn_ref, gmean_ref,
                        q_ref, kt_ref, vt_ref, ka_ref, va_ref, km_ref, ga_ref, xr_ref, gr_ref, ma_ref, mr_ref):
    q, k, part = _proj_parts(x_ref, nw_ref, w_ref, qn_ref, kn_ref, gmean_ref)
    v = part(2)
    q_ref[...] = q
    for i, ref in zip(range(3, 8), (ga_ref, xr_ref, gr_ref, ma_ref, mr_ref)):
        ref[...] = part(i)
    km_ref[...] = jnp.mean(k, axis=0, keepdims=True)
    kt = jnp.transpose(k)
    kt_ref[...] = kt
    vt_ref[...] = jnp.transpose(v)
    n = pl.program_id(0) % n_blocks
    r = lax.broadcasted_iota(jnp.int32, (HEAD_DIM, MOBA_BLOCK), 0)
    j = lax.broadcasted_iota(jnp.int32, (HEAD_DIM, MOBA_BLOCK), 1).astype(F32)
    ones_rows = (r == n) | (r == n_blocks + AUG_ROW_ONES_I) | (r == n_blocks + AUG_ROW_ONES_QB)
    lane = lax.broadcasted_iota(jnp.int32, (MOBA_BLOCK, LANES), 1)
    for h in range(N_HEADS):
        slope = 2.0 ** (-8.0 * (h + 1) / N_HEADS)
        extra = jnp.where(ones_rows, 1.0, 0.0)
        extra = jnp.where(r == n_blocks + AUG_ROW_KEY, slope * j, extra)
        extra = jnp.where(r == n_blocks + AUG_ROW_BLOCK, slope * (n * MOBA_BLOCK).astype(F32), extra)
        ka_ref[h] = jnp.concatenate([kt[h * HEAD_DIM:(h + 1) * HEAD_DIM].astype(BF16), extra.astype(BF16)], axis=0)
        slab = v[:, (h // HEADS_PER_SLAB) * LANES:(h // HEADS_PER_SLAB + 1) * LANES]
        own = (lane // HEAD_DIM) == (h % HEADS_PER_SLAB)
        va_ref[h] = jnp.where(own, slab, 1.0).astype(BF16)


def _proj_in_specs(tm):
    def const(shape):
        return pl.BlockSpec(shape, lambda i: (0,) * len(shape))

    return [pl.BlockSpec((tm, D_MODEL), lambda i: (i, 0)), const((1, D_MODEL)), const((D_MODEL, N_IN)),
            const((1, D_ATT)), const((1, D_ATT)), const((D_ATT, D_ATT))]


def _proj_sample(x, norm_w, w_in_b, qn_t, kn_t, gmean):
    rows = x.shape[0]
    widths = (D_ATT, D_ATT, D_ATT, D_ATT, D_RNN, D_RNN, D_MODEL, D_MODEL)
    return pl.pallas_call(
        _proj_sample_kernel,
        grid=(1,),
        in_specs=_proj_in_specs(rows),
        out_specs=[pl.BlockSpec((rows, w), lambda i: (i, 0)) for w in widths],
        out_shape=[jax.ShapeDtypeStruct((rows, w), F32) for w in widths],
        compiler_params=pltpu.CompilerParams(dimension_semantics=("arbitrary",), vmem_limit_bytes=VMEM_LIMIT),
        name="proj_sample",
    )(x, norm_w, w_in_b, qn_t, kn_t, gmean)


def _proj_prompt(x, batch, norm_w, w_in_b, qn_t, kn_t, gmean):
    rows = x.shape[0]
    seq = rows // batch
    tm = MOBA_BLOCK
    n_blocks = seq // tm
    assert seq % tm == 0 and n_blocks + AUG_ROWS <= HEAD_DIM

    def rowspec(width):
        return pl.BlockSpec((tm, width), lambda i: (i, 0))

    def tspec():
        return pl.BlockSpec((None, D_ATT, tm), lambda i: (i // n_blocks, 0, i % n_blocks))

    widths = (D_ATT, D_RNN, D_RNN, D_MODEL, D_MODEL)
    out_specs = [rowspec(D_ATT), tspec(), tspec(),
                 pl.BlockSpec((None, N_HEADS, 2 * HEAD_DIM, tm), lambda i: (i, 0, 0, 0)),
                 pl.BlockSpec((None, N_HEADS, tm, LANES), lambda i: (i // n_blocks, 0, i % n_blocks, 0)),
                 pl.BlockSpec((None, 1, D_ATT), lambda i: (i, 0, 0))] + [rowspec(w) for w in widths]
    out_shape = [jax.ShapeDtypeStruct((rows, D_ATT), F32),
                 jax.ShapeDtypeStruct((batch, D_ATT, seq), F32),
                 jax.ShapeDtypeStruct((batch, D_ATT, seq), F32),
                 jax.ShapeDtypeStruct((batch * n_blocks, N_HEADS, 2 * HEAD_DIM, tm), BF16),
                 jax.ShapeDtypeStruct((batch, N_HEADS, seq, LANES), BF16),
                 jax.ShapeDtypeStruct((batch * n_blocks, 1, D_ATT), F32)] + [
                     jax.ShapeDtypeStruct((rows, w), F32) for w in widths]
    return pl.pallas_call(
        functools.partial(_proj_prompt_kernel, n_blocks),
        grid=(rows // tm,),
        in_specs=_proj_in_specs(tm),
        out_specs=out_specs,
        out_shape=out_shape,
        compiler_params=pltpu.CompilerParams(dimension_semantics=("arbitrary",), vmem_limit_bytes=VMEM_LIMIT),
        name="proj_prompt",
    )(x, norm_w, w_in_b, qn_t, kn_t, gmean)


def _top_blocks_t(gate, n_sel):
    sel = jnp.zeros(gate.shape, jnp.bool_)
    blk = lax.broadcasted_iota(jnp.int32, gate.shape, 0).astype(F32)
    for _ in range(n_sel):
        m = jnp.max(gate, axis=0, keepdims=True)
        idx = jnp.min(jnp.where(gate == m, blk, float(gate.shape[0])), axis=0, keepdims=True)
        hit = blk == idx
        sel = jnp.logical_or(sel, jnp.logical_and(hit, m > -jnp.inf))
        gate = jnp.where(hit, -jnp.inf, gate)
    return sel


def _moba_prompt_kernel(slopes_ref, q_ref, ka_ref, va_ref, km_ref, o_ref,
                        qa_past_ref, qa_own_ref, s_ref, p_ref, m_ref, acc_ref):
    qb = pl.program_id(2)
    n_blocks = km_ref.shape[0]
    tq = MOBA_BLOCK
    lane = lax.broadcasted_iota(jnp.int32, (tq, LANES), 1)
    lane_all = lax.broadcasted_iota(jnp.int32, (tq, HEADS_PER_STEP * HEAD_DIM), 1)
    blk = lax.broadcasted_iota(jnp.int32, (n_blocks, tq), 0)
    xrow = lax.broadcasted_iota(jnp.int32, (HEAD_DIM - n_blocks, tq), 0)
    qpos = lax.broadcasted_iota(jnp.int32, (HEAD_DIM - n_blocks, tq), 1).astype(F32)
    q = q_ref[...]
    km = km_ref[...]
    qb_f = (qb * tq).astype(F32)

    for hh in range(HEADS_PER_STEP):
        slope = slopes_ref[pl.program_id(1) * HEADS_PER_STEP + hh]
        qm = jnp.where((lane_all // HEAD_DIM) == hh, q, 0.0)
        gate = _dot_nt_precise(km, qm)
        gate = jnp.where(blk < qb, gate, -jnp.inf)
        sel = _top_blocks_t(gate, min(MOBA_TOPK, n_blocks - 1))
        bias_t = jnp.where(sel, 0.0, MASK_BIAS)
        extra_t = jnp.where(xrow == AUG_ROW_ONES_I, -slope * qpos, 0.0)
        extra_t = jnp.where(xrow == AUG_ROW_ONES_QB, -slope * qb_f, extra_t)
        extra_t = jnp.where((xrow == AUG_ROW_KEY) | (xrow == AUG_ROW_BLOCK), 1.0, extra_t)
        cols = jnp.transpose(jnp.concatenate([jnp.zeros((HEAD_DIM, tq), F32), bias_t, extra_t], axis=0))
        q_pair = q[:, (hh // 2) * LANES:(hh // 2 + 1) * LANES]
        q_low = q_pair if hh % 2 == 0 else pltpu.roll(q_pair, HEAD_DIM, 1)
        qa = jnp.where(lane < HEAD_DIM, q_low, cols)
        qa_past_ref[hh] = qa.astype(BF16)
        qa_own_ref[hh] = jnp.where(lane == HEAD_DIM + qb, 0.0, qa).astype(BF16)
        m_ref[hh] = jnp.full((tq, LANES), -jnp.inf, F32)
        acc_ref[hh] = jnp.zeros((tq, LANES), F32)

    def update(qa_ref, n0, n_blk, causal):
        off = pl.multiple_of(n0 * tq, tq)
        tiles = n_blk * tq // LANES
        for hh in range(HEADS_PER_STEP):
            for b in range(n_blk):
                s_ref[hh, :, b * tq:(b + 1) * tq] = jnp.dot(qa_ref[hh], ka_ref[n0 + b, hh],
                                                            preferred_element_type=F32)
            for c in range(tq // ROW_CHUNK):
                rows = slice(c * ROW_CHUNK, (c + 1) * ROW_CHUNK)
                m_old = m_ref[hh, rows, :]
                s = [s_ref[hh, rows, t * LANES:(t + 1) * LANES] for t in range(tiles)]
                if causal:
                    key = lax.broadcasted_iota(jnp.int32, (ROW_CHUNK, LANES), 1)
                    qry = lax.broadcasted_iota(jnp.int32, (ROW_CHUNK, LANES), 0) + c * ROW_CHUNK
                    s = [jnp.where(key + t * LANES <= qry, s[t], -jnp.inf) for t in range(tiles)]
                m_new = jnp.maximum(m_old, jnp.max(functools.reduce(jnp.maximum, s), axis=-1, keepdims=True))
                for t in range(tiles):
                    p_ref[hh, rows, t * LANES:(t + 1) * LANES] = jnp.exp(s[t] - m_new).astype(BF16)
                acc_ref[hh, rows, :] = acc_ref[hh, rows, :] * jnp.exp(m_old - m_new)
                m_ref[hh, rows, :] = m_new
            acc_ref[hh] += jnp.dot(p_ref[hh, :, :n_blk * tq], va_ref[hh, pl.ds(off, n_blk * tq), :],
                                   preferred_element_type=F32)

    update(qa_own_ref, qb, 1, True)

    def body(i, carry):
        update(qa_past_ref, 2 * i, 2, False)
        return carry

    lax.fori_loop(0, (qb + 1) // 2, body, 0)
    for hh in range(HEADS_PER_STEP):
        acc = acc_ref[hh]
        o_h = acc / pltpu.roll(acc, HEAD_DIM, 1)
        if hh % 2 == 1:
            pair = jnp.where(lane < HEAD_DIM, o_prev, o_h)
            o_ref[:, (hh // 2) * LANES:(hh // 2 + 1) * LANES] = pair
        o_prev = o_h


def _moba_prompt(slopes, q, ka, va, kmean):
    batch, seq, _ = q.shape
    n_blocks = seq // MOBA_BLOCK
    assert seq % MOBA_BLOCK == 0 and n_blocks > MOBA_TOPK and n_blocks % 2 == 0
    assert HEADS_PER_SLAB == 2 and HEADS_PER_STEP % 2 == 0 and N_HEADS % HEADS_PER_STEP == 0
    tq = MOBA_BLOCK
    width = HEADS_PER_STEP * HEAD_DIM
    return pl.pallas_call(
        _moba_prompt_kernel,
        grid=(batch, N_HEADS // HEADS_PER_STEP, n_blocks),
        in_specs=[pl.BlockSpec(memory_space=pltpu.SMEM),
                  pl.BlockSpec((None, tq, width), lambda b, g, i: (b, i, g)),
                  pl.BlockSpec((None, n_blocks, HEADS_PER_STEP, 2 * HEAD_DIM, tq), lambda b, g, i: (b, 0, g, 0, 0)),
                  pl.BlockSpec((None, HEADS_PER_STEP, seq, LANES), lambda b, g, i: (b, g, 0, 0)),
                  pl.BlockSpec((None, n_blocks, width), lambda b, g, i: (b, 0, g))],
        out_specs=pl.BlockSpec((None, tq, width), lambda b, g, i: (b, i, g)),
        out_shape=jax.ShapeDtypeStruct((batch, seq, D_ATT), F32),
        scratch_shapes=[pltpu.VMEM((HEADS_PER_STEP, tq, 2 * HEAD_DIM), BF16),
                        pltpu.VMEM((HEADS_PER_STEP, tq, 2 * HEAD_DIM), BF16),
                        pltpu.VMEM((HEADS_PER_STEP, tq, 2 * tq), F32),
                        pltpu.VMEM((HEADS_PER_STEP, tq, 2 * tq), BF16),
                        pltpu.VMEM((HEADS_PER_STEP, tq, LANES), F32),
                        pltpu.VMEM((HEADS_PER_STEP, tq, LANES), F32)],
        compiler_params=pltpu.CompilerParams(dimension_semantics=("arbitrary",) * 3, vmem_limit_bytes=VMEM_LIMIT),
        name="moba_prompt",
    )(slopes, q, ka, va, kmean)


def _moba_sample_kernel(n_past_blocks, past_len, pt_ref, q_ref, kn_ref, vnt_ref, slope_ref, tpos_ref, hmask_ref,
                        *rest):
    k_pages = rest[:PAGES_PER_STEP]
    v_pages = rest[PAGES_PER_STEP:2 * PAGES_PER_STEP]
    o_ref, o_all, m_all, l_all, g_all = rest[2 * PAGES_PER_STEP:]
    del pt_ref
    j = pl.program_id(1)
    n_tok = q_ref.shape[0]
    n_rows = n_tok * N_HEADS
    hmask = hmask_ref[...]
    slope = slope_ref[...]
    tpos = tpos_ref[...]
    q = q_ref[...]
    qbd = jnp.concatenate([jnp.broadcast_to(q[t:t + 1, :], (N_HEADS, D_ATT)) for t in range(n_tok)]
                          + [jnp.zeros((SAMPLE_ROWS - n_rows, D_ATT), F32)], axis=0) * hmask
    q_hi, q_lo = _hilo(qbd)
    colk = lax.broadcasted_iota(jnp.int32, (SAMPLE_ROWS, MOBA_BLOCK), 1).astype(F32)
    colb = lax.broadcasted_iota(jnp.int32, (SAMPLE_ROWS, LANES), 1)

    @pl.when(j == 0)
    def _():
        m_all[...] = jnp.full(m_all.shape, -jnp.inf, F32)
        l_all[...] = jnp.zeros(l_all.shape, F32)
        g_all[...] = jnp.full(g_all.shape, -jnp.inf, F32)

    for i in range(BLOCKS_PER_STEP):
        n = j * BLOCKS_PER_STEP + i
        pages = range(i * PAGES_PER_BLOCK, (i + 1) * PAGES_PER_BLOCK)
        kt = jnp.concatenate([k_pages[p][...] for p in pages], axis=1).astype(BF16)
        vt = jnp.concatenate([v_pages[p][...] for p in pages], axis=1).astype(BF16)
        raw = jnp.dot(q_hi, kt, preferred_element_type=F32) + jnp.dot(q_lo, kt, preferred_element_type=F32)
        gate = jnp.mean(raw, axis=-1, keepdims=True)
        dist = (past_len + tpos - (n * MOBA_BLOCK).astype(F32)) - colk
        s = raw - slope * dist
        m = jnp.max(s, axis=-1, keepdims=True)
        p = jnp.exp(s - m)
        l = jnp.sum(p, axis=-1, keepdims=True)
        o_all[n] = _dot_nt(vt, p.astype(BF16))
        here = colb == n
        m_all[...] = jnp.where(here, m, m_all[...])
        l_all[...] = jnp.where(here, l, l_all[...])
        g_all[...] = jnp.where(here, gate, g_all[...])

    @pl.when(j == pl.num_programs(1) - 1)
    def _():
        sel = _top_blocks(g_all[...], colb, min(MOBA_TOPK, n_past_blocks))
        kn = kn_ref[...]
        s_own = []
        for t in range(n_tok):
            s_t = jnp.sum(qbd * kn[t:t + 1, :], axis=-1, keepdims=True) - slope * (tpos - float(t))
            s_own.append(jnp.where(tpos >= float(t), s_t, -jnp.inf))
        m_own = functools.reduce(jnp.maximum, s_own)
        m_sel = jnp.max(jnp.where(sel, m_all[...], -jnp.inf), axis=-1, keepdims=True)
        m_tot = jnp.maximum(m_own, m_sel)
        w = jnp.where(sel, jnp.exp(m_all[...] - m_tot), 0.0)
        l_tot = jnp.sum(w * l_all[...], axis=-1, keepdims=True)
        for t in range(n_tok):
            p_t = jnp.exp(s_own[t] - m_tot)
            l_tot = l_tot + p_t
            w = jnp.where(colb == n_past_blocks + t, p_t, w)
        wt = jnp.transpose(w / l_tot)
        acc = jnp.zeros((D_ATT, SAMPLE_ROWS), F32)
        for n in range(n_past_blocks):
            acc = acc + o_all[n] * wt[n:n + 1, :]
        vnt = vnt_ref[...]
        for t in range(n_tok):
            acc = acc + vnt[:, t:t + 1] * wt[n_past_blocks + t:n_past_blocks + t + 1, :]
        out = jnp.transpose(acc)[:n_rows] * hmask[:n_rows]
        o_ref[...] = jnp.sum(out.reshape(n_tok, N_HEADS, D_ATT), axis=1)


def _moba_sample(layer, page_table, q, k_new, v_new_t, cache_kt, cache_vt, slope_rows, tpos_rows, hmask, past_len):
    dec_batch, n_tok, _ = q.shape
    n_pages = page_table.shape[1]
    assert past_len % MOBA_BLOCK == 0 and n_pages * PAGE_SIZE == past_len and n_pages % PAGES_PER_STEP == 0
    n_past_blocks = past_len // MOBA_BLOCK
    assert SAMPLE_ROWS == LANES and n_tok * N_HEADS <= SAMPLE_ROWS and 0 < n_past_blocks <= LANES - n_tok
    n_steps = n_pages // PAGES_PER_STEP

    def tokspec():
        return pl.BlockSpec((None, n_tok, D_ATT), lambda b, j, pt: (b, 0, 0))

    def const(shape):
        return pl.BlockSpec(shape, lambda b, j, pt: (0,) * len(shape))

    def pagespec(i):
        return pl.BlockSpec((None, None, D_ATT, PAGE_SIZE),
                            lambda b, j, pt: (layer, pt[b * n_pages + j * PAGES_PER_STEP + i], 0, 0))

    grid_spec = pltpu.PrefetchScalarGridSpec(
        num_scalar_prefetch=1,
        grid=(dec_batch, n_steps),
        in_specs=[tokspec(), tokspec(), pl.BlockSpec((None, D_ATT, n_tok), lambda b, j, pt: (b, 0, 0)),
                  const((SAMPLE_ROWS, 1)), const((SAMPLE_ROWS, 1)), const((SAMPLE_ROWS, D_ATT))]
        + [pagespec(i) for i in range(PAGES_PER_STEP)] * 2,
        out_specs=tokspec(),
        scratch_shapes=[pltpu.VMEM((n_past_blocks, D_ATT, SAMPLE_ROWS), F32),
                        pltpu.VMEM((SAMPLE_ROWS, LANES), F32),
                        pltpu.VMEM((SAMPLE_ROWS, LANES), F32),
                        pltpu.VMEM((SAMPLE_ROWS, LANES), F32)],
    )
    return pl.pallas_call(
        functools.partial(_moba_sample_kernel, n_past_blocks, float(past_len)),
        grid_spec=grid_spec,
        out_shape=jax.ShapeDtypeStruct((dec_batch, n_tok, D_ATT), F32),
        compiler_params=pltpu.CompilerParams(dimension_semantics=("arbitrary",) * 2, vmem_limit_bytes=VMEM_LIMIT),
        name="moba_sample",
    )(page_table.reshape(-1), q, k_new, v_new_t, slope_rows, tpos_rows, hmask,
      *([cache_kt] * PAGES_PER_STEP), *([cache_vt] * PAGES_PER_STEP))


def _rglru_coeffs(xc, wa_ref, ba_ref, wx_ref, bx_ref, lam_ref):
    xb = xc.astype(BF16)

    def block_diag(w_ref):
        return jnp.concatenate(
            [jnp.dot(xb[:, n * RNN_BLOCK:(n + 1) * RNN_BLOCK], w_ref[n], preferred_element_type=F32)
             for n in range(N_RNN_BLOCKS)], axis=-1)

    r = jax.nn.sigmoid(block_diag(wa_ref) + ba_ref[...])
    gate_i = jax.nn.sigmoid(block_diag(wx_ref) + bx_ref[...])
    z = -lam_ref[...]
    softplus = jnp.maximum(z, 0.0) + jnp.log1p(jnp.exp(-jnp.abs(z)))
    log_a = -LRU_C * r * softplus
    a = jnp.exp(log_a)
    b = jnp.sqrt(1.0 - a * a) * (gate_i * xc)
    return a, b


def _rglru_prompt_kernel(x_ref, cw_ref, cb_ref, wa_ref, ba_ref, wx_ref, bx_ref, lam_ref,
                         y_ref, conv_ref, hlast_ref, xtail, hcarry):
    t = pl.program_id(1)
    tt = x_ref.shape[0]

    @pl.when(t == 0)
    def _():
        xtail[...] = jnp.zeros(xtail.shape, F32)
        hcarry[...] = jnp.zeros(hcarry.shape, F32)

    x = x_ref[...]
    tail = xtail[...]
    row = lax.broadcasted_iota(jnp.int32, (tt, D_RNN), 0)
    row8 = lax.broadcasted_iota(jnp.int32, (8, D_RNN), 0)
    xc = cb_ref[...] + x * cw_ref[CONV_W - 1:CONV_W, :]
    for d in range(1, CONV_W):
        head = jnp.where(row8 < d, pltpu.roll(tail, d, 0), pltpu.roll(x[:8], d, 0))
        xs = jnp.concatenate([head, pltpu.roll(x, d, 0)[8:]], axis=0)
        xc = xc + xs * cw_ref[CONV_W - 1 - d:CONV_W - d, :]
    a, b = _rglru_coeffs(xc, wa_ref, ba_ref, wx_ref, bx_ref, lam_ref)
    d = 1
    while d < tt:
        keep = row >= d
        b = jnp.where(keep, a * pltpu.roll(b, d, 0) + b, b)
        a = jnp.where(keep, a * pltpu.roll(a, d, 0), a)
        d *= 2
    h = a * hcarry[...] + b
    y_ref[...] = h
    xtail[...] = x[tt - 8:]
    hcarry[...] = h[tt - 1:tt]

    @pl.when(t == pl.num_programs(1) - 1)
    def _():
        conv_ref[...] = x_ref[tt - (CONV_W - 1):tt, :]
        hlast_ref[...] = h[tt - 1:tt]


def _rglru_prompt(xr, conv_w, conv_b, wa_b, b_a, wx_b, b_x, lam):
    batch, seq, _ = xr.shape
    tt = 256
    assert seq % tt == 0 and tt >= 8

    def const(shape):
        return pl.BlockSpec(shape, lambda b, t: (0,) * len(shape))

    wspec = const((N_RNN_BLOCKS, RNN_BLOCK, RNN_BLOCK))
    vec = const((1, D_RNN))
    return pl.pallas_call(
        _rglru_prompt_kernel,
        grid=(batch, seq // tt),
        in_specs=[pl.BlockSpec((None, tt, D_RNN), lambda b, t: (b, t, 0)),
                  const((CONV_W, D_RNN)), vec, wspec, vec, wspec, vec, vec],
        out_specs=[pl.BlockSpec((None, tt, D_RNN), lambda b, t: (b, t, 0)),
                   pl.BlockSpec((None, CONV_W - 1, D_RNN), lambda b, t: (b, 0, 0)),
                   pl.BlockSpec((None, 1, D_RNN), lambda b, t: (b, 0, 0))],
        out_shape=[jax.ShapeDtypeStruct((batch, seq, D_RNN), F32),
                   jax.ShapeDtypeStruct((batch, CONV_W - 1, D_RNN), F32),
                   jax.ShapeDtypeStruct((batch, 1, D_RNN), F32)],
        scratch_shapes=[pltpu.VMEM((8, D_RNN), F32), pltpu.VMEM((1, D_RNN), F32)],
        compiler_params=pltpu.CompilerParams(dimension_semantics=("arbitrary",) * 2, vmem_limit_bytes=VMEM_LIMIT),
        name="rglru_prompt",
    )(xr, conv_w, conv_b, wa_b, b_a, wx_b, b_x, lam)


def _rglru_sample_kernel(x_ref, prev_ref, h0_ref, cw_ref, cb_ref, wa_ref, ba_ref, wx_ref, bx_ref, lam_ref,
                         y_ref, hlast_ref):
    n_tok, n_seq = x_ref.shape[0], x_ref.shape[1]
    xp = [prev_ref[i] for i in range(CONV_W - 1)] + [x_ref[i] for i in range(n_tok)]
    xc = [cb_ref[...] + functools.reduce(lambda u, w: u + w, [xp[t + j] * cw_ref[j:j + 1, :] for j in range(CONV_W)])
          for t in range(n_tok)]
    a, b = _rglru_coeffs(jnp.concatenate(xc, axis=0), wa_ref, ba_ref, wx_ref, bx_ref, lam_ref)
    h = h0_ref[...]
    for t in range(n_tok):
        h = a[t * n_seq:(t + 1) * n_seq] * h + b[t * n_seq:(t + 1) * n_seq]
        y_ref[t] = h
    hlast_ref[...] = h


def _rglru_sample(xr_t, prev_t, h0, conv_w, conv_b, wa_b, b_a, wx_b, b_x, lam):
    n_tok, n_seq, _ = xr_t.shape
    assert n_seq % 8 == 0
    return pl.pallas_call(
        _rglru_sample_kernel,
        out_shape=[jax.ShapeDtypeStruct((n_tok, n_seq, D_RNN), F32), jax.ShapeDtypeStruct((n_seq, D_RNN), F32)],
        compiler_params=pltpu.CompilerParams(vmem_limit_bytes=VMEM_LIMIT),
        name="rglru_sample",
    )(xr_t, prev_t, h0, conv_w, conv_b, wa_b, b_a, wx_b, b_x, lam)


def _merge_kernel(x_ref, att_ref, ga_ref, rnn_ref, gr_ref, ma_ref, mr_ref, wao_ref, wro_ref, wo_ref, y_ref):
    y_a = jnp.dot((att_ref[...] * jax.nn.silu(ga_ref[...])).astype(BF16), wao_ref[...], preferred_element_type=F32)
    y_r = jnp.dot((rnn_ref[...] * jax.nn.silu(gr_ref[...])).astype(BF16), wro_ref[...], preferred_element_type=F32)
    z = jax.nn.sigmoid(ma_ref[...]) * y_a + jax.nn.sigmoid(mr_ref[...]) * y_r
    y_ref[...] = x_ref[...] + jnp.dot(z.astype(BF16), wo_ref[...], preferred_element_type=F32)


def _merge(x, att, ga, rnn, gr, ma, mr, wao_b, wro_b, wo_b):
    rows = x.shape[0]
    tm = min(rows, 256)
    assert rows % tm == 0

    def rowspec(width):
        return pl.BlockSpec((tm, width), lambda i: (i, 0))

    def const(shape):
        return pl.BlockSpec(shape, lambda i: (0,) * len(shape))

    return pl.pallas_call(
        _merge_kernel,
        grid=(rows // tm,),
        in_specs=[rowspec(D_MODEL), rowspec(D_ATT), rowspec(D_ATT), rowspec(D_RNN), rowspec(D_RNN),
                  rowspec(D_MODEL), rowspec(D_MODEL),
                  const((D_ATT, D_MODEL)), const((D_RNN, D_MODEL)), const((D_MODEL, D_MODEL))],
        out_specs=rowspec(D_MODEL),
        out_shape=jax.ShapeDtypeStruct((rows, D_MODEL), F32),
        compiler_params=pltpu.CompilerParams(dimension_semantics=("arbitrary",), vmem_limit_bytes=VMEM_LIMIT),
        name="merge",
    )(x, att, ga, rnn, gr, ma, mr, wao_b, wro_b, wo_b)


def kernel(x_prompt, x_sample, cache_k, cache_v, state_conv, state_h, page_table, norm_w, w_in, q_norm_w, k_norm_w,
           conv_w, conv_b, w_gate_a, b_gate_a, w_gate_x, b_gate_x, lru_lambda, w_attn_out, w_rnn_out, w_out):
    batch, seq, _ = x_prompt.shape
    dec_batch, dec_seq, _ = x_sample.shape
    depth = w_in.shape[0]
    n_pool = cache_k.shape[1]
    past_len = page_table.shape[1] * PAGE_SIZE
    n_blocks = seq // MOBA_BLOCK

    slopes = jnp.asarray([2.0 ** (-8.0 * (h + 1) / N_HEADS) for h in range(N_HEADS)], F32)
    head_of_lane = np.arange(D_ATT) // HEAD_DIM
    gmean = jnp.asarray((head_of_lane[:, None] == head_of_lane[None, :]) / HEAD_DIM, BF16)
    n_rows = dec_seq * N_HEADS
    row_head = np.arange(SAMPLE_ROWS) % N_HEADS
    row_live = np.arange(SAMPLE_ROWS) < n_rows
    slope_rows = jnp.asarray(np.where(row_live, 2.0 ** (-8.0 * (row_head + 1) / N_HEADS), 0.0), F32)[:, None]
    tpos_rows = jnp.asarray(np.where(row_live, np.arange(SAMPLE_ROWS) // N_HEADS, 0), F32)[:, None]
    hmask = jnp.asarray((row_head[:, None] == head_of_lane[None, :]) & row_live[:, None], F32)

    w_in_b = w_in.astype(BF16)
    wa_b = w_gate_a.astype(BF16)
    wx_b = w_gate_x.astype(BF16)
    wao_b = w_attn_out.astype(BF16)
    wro_b = w_rnn_out.astype(BF16)
    wo_b = w_out.astype(BF16)
    cache_kt = jnp.transpose(cache_k, (0, 1, 3, 4, 2)).reshape(depth, n_pool, D_ATT, PAGE_SIZE)
    cache_vt = jnp.transpose(cache_v, (0, 1, 3, 4, 2)).reshape(depth, n_pool, D_ATT, PAGE_SIZE)

    y_p = x_prompt.reshape(batch * seq, D_MODEL)
    y_s = x_sample.reshape(dec_batch * dec_seq, D_MODEL)
    outs = {name: [] for name in ("kp", "vp", "cp", "hp", "ks", "vs", "cs", "hs")}
    for l in range(depth):
        nw = norm_w[l][None, :]
        qn_t = jnp.tile(q_norm_w[l], N_HEADS)[None, :]
        kn_t = jnp.tile(k_norm_w[l], N_HEADS)[None, :]
        rnn_w = (conv_w[l], conv_b[l][None, :], wa_b[l], b_gate_a[l][None, :], wx_b[l], b_gate_x[l][None, :],
                 lru_lambda[l][None, :])
        out_w = (wao_b[l], wro_b[l], wo_b[l])

        q, kt, vt, ka, va, kmean, ga, xr, gr, ma, mr = _proj_prompt(y_p, batch, nw, w_in_b[l], qn_t, kn_t, gmean)
        att = _moba_prompt(slopes, q.reshape(batch, seq, D_ATT),
                           ka.reshape(batch, n_blocks, N_HEADS, 2 * HEAD_DIM, MOBA_BLOCK), va,
                           kmean.reshape(batch, n_blocks, D_ATT))
        rnn, c_new, h_new = _rglru_prompt(xr.reshape(batch, seq, D_RNN), *rnn_w)
        y_p = _merge(y_p, att.reshape(batch * seq, D_ATT), ga, rnn.reshape(batch * seq, D_RNN), gr, ma, mr, *out_w)
        outs["kp"].append(kt)
        outs["vp"].append(vt)
        outs["cp"].append(c_new)
        outs["hp"].append(h_new.reshape(batch, D_RNN))

        q, k, v, ga, xr, gr, ma, mr = _proj_sample(y_s, nw, w_in_b[l], qn_t, kn_t, gmean)
        att = _moba_sample(l, page_table, q.reshape(dec_batch, dec_seq, D_ATT), k.reshape(dec_batch, dec_seq, D_ATT),
                           jnp.swapaxes(v.reshape(dec_batch, dec_seq, D_ATT), 1, 2), cache_kt, cache_vt,
                           slope_rows, tpos_rows, hmask, past_len)
        xr_t = jnp.swapaxes(xr.reshape(dec_batch, dec_seq, D_RNN), 0, 1)
        prev_t = jnp.swapaxes(state_conv[l], 0, 1)
        rnn_t, h_new = _rglru_sample(xr_t, prev_t, state_h[l], *rnn_w)
        rnn = jnp.swapaxes(rnn_t, 0, 1).reshape(dec_batch * dec_seq, D_RNN)
        c_new = jnp.swapaxes(jnp.concatenate([prev_t, xr_t], axis=0)[dec_seq:], 0, 1)
        y_s = _merge(y_s, att.reshape(dec_batch * dec_seq, D_ATT), ga, rnn, gr, ma, mr, *out_w)
        outs["ks"].append(k.reshape(dec_batch, dec_seq, N_HEADS, HEAD_DIM))
        outs["vs"].append(v.reshape(dec_batch, dec_seq, N_HEADS, HEAD_DIM))
        outs["cs"].append(c_new)
        outs["hs"].append(h_new)

    def from_transposed(parts):
        t = jnp.stack(parts).reshape(depth, batch, N_HEADS, HEAD_DIM, seq)
        return jnp.transpose(t, (0, 1, 4, 2, 3))

    return (y_p.reshape(batch, seq, D_MODEL), y_s.reshape(dec_batch, dec_seq, D_MODEL),
            from_transposed(outs["kp"]), from_transposed(outs["vp"]), jnp.stack(outs["cp"]), jnp.stack(outs["hp"]),
            jnp.stack(outs["ks"]), jnp.stack(outs["vs"]), jnp.stack(outs["cs"]), jnp.stack(outs["hs"]))
```

```python
import functools
import math

import numpy as np
import jax
import jax.numpy as jnp
from jax import lax
from jax.experimental import pallas as pl
from jax.experimental.pallas import tpu as pltpu

F32 = jnp.float32
BF16 = jnp.bfloat16

D_MODEL = 1024
N_HEADS = 8
HEAD_DIM = 64
D_ATT = N_HEADS * HEAD_DIM
D_RNN = D_MODEL
N_RNN_BLOCKS = 8
RNN_BLOCK = D_RNN // N_RNN_BLOCKS
CONV_W = 4
LRU_C = 8.0
MOBA_BLOCK = 256
MOBA_TOPK = 3
PAGE_SIZE = 128
RMS_EPS = 1e-6
SPLITS = (D_ATT, D_ATT, D_ATT, D_ATT, D_RNN, D_RNN, D_MODEL, D_MODEL)
N_IN = sum(SPLITS)
CUTS = tuple(int(c) for c in np.cumsum((0,) + SPLITS))

LANES = 128
HEADS_PER_SLAB = LANES // HEAD_DIM
N_SLABS = D_ATT // LANES
MASK_BIAS = -1e30
VMEM_LIMIT = 56 * 1024 * 1024

PAGES_PER_STEP = 8
BLOCKS_PER_STEP = PAGES_PER_STEP * PAGE_SIZE // MOBA_BLOCK
PAGES_PER_BLOCK = MOBA_BLOCK // PAGE_SIZE
SAMPLE_ROWS = LANES

AUG_ROW_ONES_I = 0
AUG_ROW_ONES_QB = 1
AUG_ROW_KEY = 2
AUG_ROW_BLOCK = 3
AUG_ROWS = 4
ROW_CHUNK = 64
HEADS_PER_STEP = 4
assert all(math.frexp(2.0 ** (-8.0 * (h + 1) / N_HEADS))[0] == 0.5 for h in range(N_HEADS)) and MOBA_BLOCK <= 256


def _hilo(x):
    hi = x.astype(BF16)
    lo = (x - hi.astype(F32)).astype(BF16)
    return hi, lo


def _dot_nt(a, b):
    return lax.dot_general(a, b, (((1,), (1,)), ((), ())), preferred_element_type=F32)


def _top_blocks(gate, col, n_sel):
    sel = jnp.zeros(gate.shape, jnp.bool_)
    colf = col.astype(F32)
    for _ in range(n_sel):
        m = jnp.max(gate, axis=-1, keepdims=True)
        idx = jnp.min(jnp.where(gate == m, colf, float(gate.shape[-1])), axis=-1, keepdims=True)
        hit = colf == idx
        sel = jnp.logical_or(sel, jnp.logical_and(hit, m > -jnp.inf))
        gate = jnp.where(hit, -jnp.inf, gate)
    return sel


def _proj_parts(x_ref, nw_ref, w_ref, qn_ref, kn_ref, gmean_ref):
    x = x_ref[...]
    ms = jnp.mean(x * x, axis=-1, keepdims=True)
    h = (x * lax.rsqrt(ms + RMS_EPS) * nw_ref[...]).astype(BF16)

    def part(i):
        return jnp.dot(h, w_ref[:, CUTS[i]:CUTS[i + 1]], preferred_element_type=F32)

    def head_norm(z, w):
        sq_hi, sq_lo = _hilo(z * z)
        g = gmean_ref[...]
        ms_h = jnp.dot(sq_hi, g, preferred_element_type=F32) + jnp.dot(sq_lo, g, preferred_element_type=F32)
        return z * lax.rsqrt(ms_h + RMS_EPS) * w

    q = head_norm(part(0), qn_ref[...]) * (HEAD_DIM ** -0.5)
    k = head_norm(part(1), kn_ref[...])
    return q, k, part


def _proj_sample_kernel(x_ref, nw_ref, w_ref, qn_ref, kn_ref, gmean_ref,
                        q_ref, k_ref, v_ref, ga_ref, xr_ref, gr_ref, ma_ref, mr_ref):
    q, k, part = _proj_parts(x_ref, nw_ref, w_ref, qn_ref, kn_ref, gmean_ref)
    q_ref[...] = q
    k_ref[...] = k
    for i, ref in zip(range(2, 8), (v_ref, ga_ref, xr_ref, gr_ref, ma_ref, mr_ref)):
        ref[...] = part(i)


def _proj_prompt_kernel(n_blocks, x_ref, nw_ref, w_ref, qn_ref, kn_ref, gmean_ref,
                        q_ref, kt_ref, vt_ref, ka_ref, va_ref, km_ref, ga_ref, xr_ref, gr_ref, ma_ref, mr_ref):
    q, k, part = _proj_parts(x_ref, nw_ref, w_ref, qn_ref, kn_ref, gmean_ref)
    v = part(2)
    q_ref[...] = q
    for i, ref in zip(range(3, 8), (ga_ref, xr_ref, gr_ref, ma_ref, mr_ref)):
        ref[...] = part(i)
    km_ref[...] = jnp.mean(k, axis=0, keepdims=True)
    kt = jnp.transpose(k)
    kt_ref[...] = kt
    vt_ref[...] = jnp.transpose(v)
    n = pl.program_id(0) % n_blocks
    r = lax.broadcasted_iota(jnp.int32, (HEAD_DIM, MOBA_BLOCK), 0)
    j = lax.broadcasted_iota(jnp.int32, (HEAD_DIM, MOBA_BLOCK), 1).astype(F32)
    ones_rows = (r == n) | (r == n_blocks + AUG_ROW_ONES_I) | (r == n_blocks + AUG_ROW_ONES_QB)
    lane = lax.broadcasted_iota(jnp.int32, (MOBA_BLOCK, LANES), 1)
    for h in range(N_HEADS):
        slope = 2.0 ** (-8.0 * (h + 1) / N_HEADS)
        extra = jnp.where(ones_rows, 1.0, 0.0)
        extra = jnp.where(r == n_blocks + AUG_ROW_KEY, slope * j, extra)
        extra = jnp.where(r == n_blocks + AUG_ROW_BLOCK, slope * (n * MOBA_BLOCK).astype(F32), extra)
        ka_ref[h] = jnp.concatenate([kt[h * HEAD_DIM:(h + 1) * HEAD_DIM].astype(BF16), extra.astype(BF16)], axis=0)
        slab = v[:, (h // HEADS_PER_SLAB) * LANES:(h // HEADS_PER_SLAB + 1) * LANES]
        own = (lane // HEAD_DIM) == (h % HEADS_PER_SLAB)
        va_ref[h] = jnp.where(own, slab, 1.0).astype(BF16)


def _proj_in_specs(tm):
    def const(shape):
        return pl.BlockSpec(shape, lambda i: (0,) * len(shape))

    return [pl.BlockSpec((tm, D_MODEL), lambda i: (i, 0)), const((1, D_MODEL)), const((D_MODEL, N_IN)),
            const((1, D_ATT)), const((1, D_ATT)), const((D_ATT, D_ATT))]


def _proj_sample(x, norm_w, w_in_b, qn_t, kn_t, gmean):
    rows = x.shape[0]
    widths = (D_ATT, D_ATT, D_ATT, D_ATT, D_RNN, D_RNN, D_MODEL, D_MODEL)
    return pl.pallas_call(
        _proj_sample_kernel,
        grid=(1,),
        in_specs=_proj_in_specs(rows),
        out_specs=[pl.BlockSpec((rows, w), lambda i: (i, 0)) for w in widths],
        out_shape=[jax.ShapeDtypeStruct((rows, w), F32) for w in widths],
        compiler_params=pltpu.CompilerParams(dimension_semantics=("arbitrary",), vmem_limit_bytes=VMEM_LIMIT),
        name="proj_sample",
    )(x, norm_w, w_in_b, qn_t, kn_t, gmean)


def _proj_prompt(x, batch, norm_w, w_in_b, qn_t, kn_t, gmean):
    rows = x.shape[0]
    seq = rows // batch
    tm = MOBA_BLOCK
    n_blocks = seq // tm
    assert seq % tm == 0 and n_blocks + AUG_ROWS <= HEAD_DIM

    def rowspec(width):
        return pl.BlockSpec((tm, width), lambda i: (i, 0))

    def tspec():
        return pl.BlockSpec((None, D_ATT, tm), lambda i: (i // n_blocks, 0, i % n_blocks))

    widths = (D_ATT, D_RNN, D_RNN, D_MODEL, D_MODEL)
    out_specs = [rowspec(D_ATT), tspec(), tspec(),
                 pl.BlockSpec((None, N_HEADS, 2 * HEAD_DIM, tm), lambda i: (i, 0, 0, 0)),
                 pl.BlockSpec((None, N_HEADS, tm, LANES), lambda i: (i // n_blocks, 0, i % n_blocks, 0)),
                 pl.BlockSpec((None, 1, D_ATT), lambda i: (i, 0, 0))] + [rowspec(w) for w in widths]
    out_shape = [jax.ShapeDtypeStruct((rows, D_ATT), F32),
                 jax.ShapeDtypeStruct((batch, D_ATT, seq), F32),
                 jax.ShapeDtypeStruct((batch, D_ATT, seq), F32),
                 jax.ShapeDtypeStruct((batch * n_blocks, N_HEADS, 2 * HEAD_DIM, tm), BF16),
                 jax.ShapeDtypeStruct((batch, N_HEADS, seq, LANES), BF16),
                 jax.ShapeDtypeStruct((batch * n_blocks, 1, D_ATT), F32)] + [
                     jax.ShapeDtypeStruct((rows, w), F32) for w in widths]
    return pl.pallas_call(
        functools.partial(_proj_prompt_kernel, n_blocks),
        grid=(rows // tm,),
        in_specs=_proj_in_specs(tm),
        out_specs=out_specs,
        out_shape=out_shape,
        compiler_params=pltpu.CompilerParams(dimension_semantics=("arbitrary",), vmem_limit_bytes=VMEM_LIMIT),
        name="proj_prompt",
    )(x, norm_w, w_in_b, qn_t, kn_t, gmean)


def _top_blocks_t(gate, n_sel):
    sel = jnp.zeros(gate.shape, jnp.bool_)
    blk = lax.broadcasted_iota(jnp.int32, gate.shape, 0).astype(F32)
    for _ in range(n_sel):
        m = jnp.max(gate, axis=0, keepdims=True)
        idx = jnp.min(jnp.where(gate == m, blk, float(gate.shape[0])), axis=0, keepdims=True)
        hit = blk == idx
        sel = jnp.logical_or(sel, jnp.logical_and(hit, m > -jnp.inf))
        gate = jnp.where(hit, -jnp.inf, gate)
    return sel


def _moba_prompt_kernel(slopes_ref, q_ref, ka_ref, va_ref, km_ref, o_ref,
                        qa_past_ref, qa_own_ref, s_ref, p_ref, m_ref, acc_ref):
    qb = pl.program_id(2)
    n_blocks = km_ref.shape[0]
    tq = MOBA_BLOCK
    lane = lax.broadcasted_iota(jnp.int32, (tq, LANES), 1)
    lane_km = lax.broadcasted_iota(jnp.int32, (n_blocks, HEADS_PER_STEP * HEAD_DIM), 1)
    blk = lax.broadcasted_iota(jnp.int32, (n_blocks, tq), 0)
    xrow = lax.broadcasted_iota(jnp.int32, (HEAD_DIM - n_blocks, tq), 0)
    qpos = lax.broadcasted_iota(jnp.int32, (HEAD_DIM - n_blocks, tq), 1).astype(F32)
    q = q_ref[...]
    q_hi, q_lo = _hilo(q)
    km = km_ref[...]
    qb_f = (qb * tq).astype(F32)

    for hh in range(HEADS_PER_STEP):
        slope = slopes_ref[pl.program_id(1) * HEADS_PER_STEP + hh]
        km_hi, km_lo = _hilo(jnp.where((lane_km // HEAD_DIM) == hh, km, 0.0))
        gate = _dot_nt(km_hi, q_hi) + _dot_nt(km_hi, q_lo) + _dot_nt(km_lo, q_hi)
        gate = jnp.where(blk < qb, gate, -jnp.inf)
        sel = _top_blocks_t(gate, min(MOBA_TOPK, n_blocks - 1))
        bias_t = jnp.where(sel, 0.0, MASK_BIAS)
        extra_t = jnp.where(xrow == AUG_ROW_ONES_I, -slope * qpos, 0.0)
        extra_t = jnp.where(xrow == AUG_ROW_ONES_QB, -slope * qb_f, extra_t)
        extra_t = jnp.where((xrow == AUG_ROW_KEY) | (xrow == AUG_ROW_BLOCK), 1.0, extra_t)
        cols = jnp.transpose(jnp.concatenate([jnp.zeros((HEAD_DIM, tq), F32), bias_t, extra_t], axis=0))
        q_pair = q[:, (hh // 2) * LANES:(hh // 2 + 1) * LANES]
        q_low = q_pair if hh % 2 == 0 else pltpu.roll(q_pair, HEAD_DIM, 1)
        qa = jnp.where(lane < HEAD_DIM, q_low, cols)
        qa_past_ref[hh] = qa.astype(BF16)
        qa_own_ref[hh] = jnp.where(lane == HEAD_DIM + qb, 0.0, qa).astype(BF16)
        m_ref[hh] = jnp.full((tq, LANES), -jnp.inf, F32)
        acc_ref[hh] = jnp.zeros((tq, LANES), F32)

    def update(qa_ref, n0, n_blk, causal):
        off = pl.multiple_of(n0 * tq, tq)
        tiles = n_blk * tq // LANES

        def scores(hh):
            for b in range(n_blk):
                s_ref[hh, :, b * tq:(b + 1) * tq] = jnp.dot(qa_ref[hh], ka_ref[n0 + b, hh],
                                                            preferred_element_type=F32)

        def softmax(hh):
            for c in range(tq // ROW_CHUNK):
                rows = slice(c * ROW_CHUNK, (c + 1) * ROW_CHUNK)
                m_old = m_ref[hh, rows, :]
                s = [s_ref[hh, rows, t * LANES:(t + 1) * LANES] for t in range(tiles)]
                if causal:
                    key = lax.broadcasted_iota(jnp.int32, (ROW_CHUNK, LANES), 1)
                    qry = lax.broadcasted_iota(jnp.int32, (ROW_CHUNK, LANES), 0) + c * ROW_CHUNK
                    s = [jnp.where(key + t * LANES <= qry, s[t], -jnp.inf) for t in range(tiles)]
                m_new = jnp.maximum(m_old, jnp.max(functools.reduce(jnp.maximum, s), axis=-1, keepdims=True))
                for t in range(tiles):
                    p_ref[hh, rows, t * LANES:(t + 1) * LANES] = jnp.exp(s[t] - m_new).astype(BF16)
                acc_ref[hh, rows, :] = acc_ref[hh, rows, :] * jnp.exp(m_old - m_new)
                m_ref[hh, rows, :] = m_new

        def values(hh):
            acc_ref[hh] += jnp.dot(p_ref[hh, :, :n_blk * tq], va_ref[hh, pl.ds(off, n_blk * tq), :],
                                   preferred_element_type=F32)

        scores(0)
        for hh in range(HEADS_PER_STEP):
            if hh + 1 < HEADS_PER_STEP:
                scores(hh + 1)
            softmax(hh)
            values(hh)

    update(qa_own_ref, qb, 1, True)

    def body(i, carry):
        update(qa_past_ref, 2 * i, 2, False)
        return carry

    lax.fori_loop(0, (qb + 1) // 2, body, 0)
    for hh in range(HEADS_PER_STEP):
        acc = acc_ref[hh]
        o_h = acc / pltpu.roll(acc, HEAD_DIM, 1)
        if hh % 2 == 1:
            pair = jnp.where(lane < HEAD_DIM, o_prev, o_h)
            o_ref[:, (hh // 2) * LANES:(hh // 2 + 1) * LANES] = pair
        o_prev = o_h


def _moba_prompt(slopes, q, ka, va, kmean):
    batch, seq, _ = q.shape
    n_blocks = seq // MOBA_BLOCK
    assert seq % MOBA_BLOCK == 0 and n_blocks > MOBA_TOPK and n_blocks % 2 == 0
    assert HEADS_PER_SLAB == 2 and HEADS_PER_STEP % 2 == 0 and N_HEADS % HEADS_PER_STEP == 0
    tq = MOBA_BLOCK
    width = HEADS_PER_STEP * HEAD_DIM
    return pl.pallas_call(
        _moba_prompt_kernel,
        grid=(batch, N_HEADS // HEADS_PER_STEP, n_blocks),
        in_specs=[pl.BlockSpec(memory_space=pltpu.SMEM),
                  pl.BlockSpec((None, tq, width), lambda b, g, i: (b, i, g)),
                  pl.BlockSpec((None, n_blocks, HEADS_PER_STEP, 2 * HEAD_DIM, tq), lambda b, g, i: (b, 0, g, 0, 0)),
                  pl.BlockSpec((None, HEADS_PER_STEP, seq, LANES), lambda b, g, i: (b, g, 0, 0)),
                  pl.BlockSpec((None, n_blocks, width), lambda b, g, i: (b, 0, g))],
        out_specs=pl.BlockSpec((None, tq, width), lambda b, g, i: (b, i, g)),
        out_shape=jax.ShapeDtypeStruct((batch, seq, D_ATT), F32),
        scratch_shapes=[pltpu.VMEM((HEADS_PER_STEP, tq, 2 * HEAD_DIM), BF16),
                        pltpu.VMEM((HEADS_PER_STEP, tq, 2 * HEAD_DIM), BF16),
                        pltpu.VMEM((HEADS_PER_STEP, tq, 2 * tq), F32),
                        pltpu.VMEM((HEADS_PER_STEP, tq, 2 * tq), BF16),
                        pltpu.VMEM((HEADS_PER_STEP, tq, LANES), F32),
                        pltpu.VMEM((HEADS_PER_STEP, tq, LANES), F32)],
        compiler_params=pltpu.CompilerParams(dimension_semantics=("arbitrary",) * 3, vmem_limit_bytes=VMEM_LIMIT),
        name="moba_prompt",
    )(slopes, q, ka, va, kmean)


def _moba_sample_kernel(n_past_blocks, past_len, pt_ref, q_ref, kn_ref, vnt_ref, slope_ref, tpos_ref, hmask_ref,
                        *rest):
    k_pages = rest[:PAGES_PER_STEP]
    v_pages = rest[PAGES_PER_STEP:2 * PAGES_PER_STEP]
    o_ref, o_all, m_all, l_all, g_all = rest[2 * PAGES_PER_STEP:]
    del pt_ref
    j = pl.program_id(1)
    n_tok = q_ref.shape[0]
    n_rows = n_tok * N_HEADS
    hmask = hmask_ref[...]
    slope = slope_ref[...]
    tpos = tpos_ref[...]
    q = q_ref[...]
    qbd = jnp.concatenate([jnp.broadcast_to(q[t:t + 1, :], (N_HEADS, D_ATT)) for t in range(n_tok)]
                          + [jnp.zeros((SAMPLE_ROWS - n_rows, D_ATT), F32)], axis=0) * hmask
    q_hi, q_lo = _hilo(qbd)
    colk = lax.broadcasted_iota(jnp.int32, (SAMPLE_ROWS, MOBA_BLOCK), 1).astype(F32)
    colb = lax.broadcasted_iota(jnp.int32, (SAMPLE_ROWS, LANES), 1)

    @pl.when(j == 0)
    def _():
        m_all[...] = jnp.full(m_all.shape, -jnp.inf, F32)
        l_all[...] = jnp.zeros(l_all.shape, F32)
        g_all[...] = jnp.full(g_all.shape, -jnp.inf, F32)

    for i in range(BLOCKS_PER_STEP):
        n = j * BLOCKS_PER_STEP + i
        pages = range(i * PAGES_PER_BLOCK, (i + 1) * PAGES_PER_BLOCK)
        kt = jnp.concatenate([k_pages[p][...] for p in pages], axis=1).astype(BF16)
        vt = jnp.concatenate([v_pages[p][...] for p in pages], axis=1).astype(BF16)
        raw = jnp.dot(q_hi, kt, preferred_element_type=F32) + jnp.dot(q_lo, kt, preferred_element_type=F32)
        gate = jnp.mean(raw, axis=-1, keepdims=True)
        dist = (past_len + tpos - (n * MOBA_BLOCK).astype(F32)) - colk
        s = raw - slope * dist
        m = jnp.max(s, axis=-1, keepdims=True)
        p = jnp.exp(s - m)
        l = jnp.sum(p, axis=-1, keepdims=True)
        o_all[n] = _dot_nt(vt, p.astype(BF16))
        here = colb == n
        m_all[...] = jnp.where(here, m, m_all[...])
        l_all[...] = jnp.where(here, l, l_all[...])
        g_all[...] = jnp.where(here, gate, g_all[...])

    @pl.when(j == pl.num_programs(1) - 1)
    def _():
        sel = _top_blocks(g_all[...], colb, min(MOBA_TOPK, n_past_blocks))
        kn = kn_ref[...]
        s_own = []
        for t in range(n_tok):
            s_t = jnp.sum(qbd * kn[t:t + 1, :], axis=-1, keepdims=True) - slope * (tpos - float(t))
            s_own.append(jnp.where(tpos >= float(t), s_t, -jnp.inf))
        m_own = functools.reduce(jnp.maximum, s_own)
        m_sel = jnp.max(jnp.where(sel, m_all[...], -jnp.inf), axis=-1, keepdims=True)
        m_tot = jnp.maximum(m_own, m_sel)
        w = jnp.where(sel, jnp.exp(m_all[...] - m_tot), 0.0)
        l_tot = jnp.sum(w * l_all[...], axis=-1, keepdims=True)
        for t in range(n_tok):
            p_t = jnp.exp(s_own[t] - m_tot)
            l_tot = l_tot + p_t
            w = jnp.where(colb == n_past_blocks + t, p_t, w)
        wt = jnp.transpose(w / l_tot)
        acc = jnp.zeros((D_ATT, SAMPLE_ROWS), F32)
        for n in range(n_past_blocks):
            acc = acc + o_all[n] * wt[n:n + 1, :]
        vnt = vnt_ref[...]
        for t in range(n_tok):
            acc = acc + vnt[:, t:t + 1] * wt[n_past_blocks + t:n_past_blocks + t + 1, :]
        out = jnp.transpose(acc)[:n_rows] * hmask[:n_rows]
        o_ref[...] = jnp.sum(out.reshape(n_tok, N_HEADS, D_ATT), axis=1)


def _moba_sample(layer, page_table, q, k_new, v_new_t, cache_kt, cache_vt, slope_rows, tpos_rows, hmask, past_len):
    dec_batch, n_tok, _ = q.shape
    n_pages = page_table.shape[1]
    assert past_len % MOBA_BLOCK == 0 and n_pages * PAGE_SIZE == past_len and n_pages % PAGES_PER_STEP == 0
    n_past_blocks = past_len // MOBA_BLOCK
    assert SAMPLE_ROWS == LANES and n_tok * N_HEADS <= SAMPLE_ROWS and 0 < n_past_blocks <= LANES - n_tok
    n_steps = n_pages // PAGES_PER_STEP

    def tokspec():
        return pl.BlockSpec((None, n_tok, D_ATT), lambda b, j, pt: (b, 0, 0))

    def const(shape):
        return pl.BlockSpec(shape, lambda b, j, pt: (0,) * len(shape))

    def pagespec(i):
        return pl.BlockSpec((None, None, D_ATT, PAGE_SIZE),
                            lambda b, j, pt: (layer, pt[b * n_pages + j * PAGES_PER_STEP + i], 0, 0))

    grid_spec = pltpu.PrefetchScalarGridSpec(
        num_scalar_prefetch=1,
        grid=(dec_batch, n_steps),
        in_specs=[tokspec(), tokspec(), pl.BlockSpec((None, D_ATT, n_tok), lambda b, j, pt: (b, 0, 0)),
                  const((SAMPLE_ROWS, 1)), const((SAMPLE_ROWS, 1)), const((SAMPLE_ROWS, D_ATT))]
        + [pagespec(i) for i in range(PAGES_PER_STEP)] * 2,
        out_specs=tokspec(),
        scratch_shapes=[pltpu.VMEM((n_past_blocks, D_ATT, SAMPLE_ROWS), F32),
                        pltpu.VMEM((SAMPLE_ROWS, LANES), F32),
                        pltpu.VMEM((SAMPLE_ROWS, LANES), F32),
                        pltpu.VMEM((SAMPLE_ROWS, LANES), F32)],
    )
    return pl.pallas_call(
        functools.partial(_moba_sample_kernel, n_past_blocks, float(past_len)),
        grid_spec=grid_spec,
        out_shape=jax.ShapeDtypeStruct((dec_batch, n_tok, D_ATT), F32),
        compiler_params=pltpu.CompilerParams(dimension_semantics=("arbitrary",) * 2, vmem_limit_bytes=VMEM_LIMIT),
        name="moba_sample",
    )(page_table.reshape(-1), q, k_new, v_new_t, slope_rows, tpos_rows, hmask,
      *([cache_kt] * PAGES_PER_STEP), *([cache_vt] * PAGES_PER_STEP))


def _rglru_coeffs(xc, wa_ref, ba_ref, wx_ref, bx_ref, lam_ref):
    xb = xc.astype(BF16)

    def block_diag(w_ref):
        return jnp.concatenate(
            [jnp.dot(xb[:, n * RNN_BLOCK:(n + 1) * RNN_BLOCK], w_ref[n], preferred_element_type=F32)
             for n in range(N_RNN_BLOCKS)], axis=-1)

    r = jax.nn.sigmoid(block_diag(wa_ref) + ba_ref[...])
    gate_i = jax.nn.sigmoid(block_diag(wx_ref) + bx_ref[...])
    z = -lam_ref[...]
    softplus = jnp.maximum(z, 0.0) + jnp.log1p(jnp.exp(-jnp.abs(z)))
    log_a = -LRU_C * r * softplus
    a = jnp.exp(log_a)
    b = jnp.sqrt(1.0 - a * a) * (gate_i * xc)
    return a, b


def _rglru_prompt_kernel(x_ref, cw_ref, cb_ref, wa_ref, ba_ref, wx_ref, bx_ref, lam_ref,
                         y_ref, conv_ref, hlast_ref, xtail, hcarry):
    t = pl.program_id(1)
    tt = x_ref.shape[0]

    @pl.when(t == 0)
    def _():
        xtail[...] = jnp.zeros(xtail.shape, F32)
        hcarry[...] = jnp.zeros(hcarry.shape, F32)

    x = x_ref[...]
    tail = xtail[...]
    row = lax.broadcasted_iota(jnp.int32, (tt, D_RNN), 0)
    row8 = lax.broadcasted_iota(jnp.int32, (8, D_RNN), 0)
    xc = cb_ref[...] + x * cw_ref[CONV_W - 1:CONV_W, :]
    for d in range(1, CONV_W):
        head = jnp.where(row8 < d, pltpu.roll(tail, d, 0), pltpu.roll(x[:8], d, 0))
        xs = jnp.concatenate([head, pltpu.roll(x, d, 0)[8:]], axis=0)
        xc = xc + xs * cw_ref[CONV_W - 1 - d:CONV_W - d, :]
    a, b = _rglru_coeffs(xc, wa_ref, ba_ref, wx_ref, bx_ref, lam_ref)
    d = 1
    while d < tt:
        keep = row >= d
        b = jnp.where(keep, a * pltpu.roll(b, d, 0) + b, b)
        a = jnp.where(keep, a * pltpu.roll(a, d, 0), a)
        d *= 2
    h = a * hcarry[...] + b
    y_ref[...] = h
    xtail[...] = x[tt - 8:]
    hcarry[...] = h[tt - 1:tt]

    @pl.when(t == pl.num_programs(1) - 1)
    def _():
        conv_ref[...] = x_ref[tt - (CONV_W - 1):tt, :]
        hlast_ref[...] = h[tt - 1:tt]


def _rglru_prompt(xr, conv_w, conv_b, wa_b, b_a, wx_b, b_x, lam):
    batch, seq, _ = xr.shape
    tt = 256
    assert seq % tt == 0 and tt >= 8

    def const(shape):
        return pl.BlockSpec(shape, lambda b, t: (0,) * len(shape))

    wspec = const((N_RNN_BLOCKS, RNN_BLOCK, RNN_BLOCK))
    vec = const((1, D_RNN))
    return pl.pallas_call(
        _rglru_prompt_kernel,
        grid=(batch, seq // tt),
        in_specs=[pl.BlockSpec((None, tt, D_RNN), lambda b, t: (b, t, 0)),
                  const((CONV_W, D_RNN)), vec, wspec, vec, wspec, vec, vec],
        out_specs=[pl.BlockSpec((None, tt, D_RNN), lambda b, t: (b, t, 0)),
                   pl.BlockSpec((None, CONV_W - 1, D_RNN), lambda b, t: (b, 0, 0)),
                   pl.BlockSpec((None, 1, D_RNN), lambda b, t: (b, 0, 0))],
        out_shape=[jax.ShapeDtypeStruct((batch, seq, D_RNN), F32),
                   jax.ShapeDtypeStruct((batch, CONV_W - 1, D_RNN), F32),
                   jax.ShapeDtypeStruct((batch, 1, D_RNN), F32)],
        scratch_shapes=[pltpu.VMEM((8, D_RNN), F32), pltpu.VMEM((1, D_RNN), F32)],
        compiler_params=pltpu.CompilerParams(dimension_semantics=("arbitrary",) * 2, vmem_limit_bytes=VMEM_LIMIT),
        name="rglru_prompt",
    )(xr, conv_w, conv_b, wa_b, b_a, wx_b, b_x, lam)


def _rglru_sample_kernel(x_ref, prev_ref, h0_ref, cw_ref, cb_ref, wa_ref, ba_ref, wx_ref, bx_ref, lam_ref,
                         y_ref, hlast_ref):
    n_tok, n_seq = x_ref.shape[0], x_ref.shape[1]
    xp = [prev_ref[i] for i in range(CONV_W - 1)] + [x_ref[i] for i in range(n_tok)]
    xc = [cb_ref[...] + functools.reduce(lambda u, w: u + w, [xp[t + j] * cw_ref[j:j + 1, :] for j in range(CONV_W)])
          for t in range(n_tok)]
    a, b = _rglru_coeffs(jnp.concatenate(xc, axis=0), wa_ref, ba_ref, wx_ref, bx_ref, lam_ref)
    h = h0_ref[...]
    for t in range(n_tok):
        h = a[t * n_seq:(t + 1) * n_seq] * h + b[t * n_seq:(t + 1) * n_seq]
        y_ref[t] = h
    hlast_ref[...] = h


def _rglru_sample(xr_t, prev_t, h0, conv_w, conv_b, wa_b, b_a, wx_b, b_x, lam):
    n_tok, n_seq, _ = xr_t.shape
    assert n_seq % 8 == 0
    return pl.pallas_call(
        _rglru_sample_kernel,
        out_shape=[jax.ShapeDtypeStruct((n_tok, n_seq, D_RNN), F32), jax.ShapeDtypeStruct((n_seq, D_RNN), F32)],
        compiler_params=pltpu.CompilerParams(vmem_limit_bytes=VMEM_LIMIT),
        name="rglru_sample",
    )(xr_t, prev_t, h0, conv_w, conv_b, wa_b, b_a, wx_b, b_x, lam)


def _merge_kernel(x_ref, att_ref, ga_ref, rnn_ref, gr_ref, ma_ref, mr_ref, wao_ref, wro_ref, wo_ref, y_ref):
    y_a = jnp.dot((att_ref[...] * jax.nn.silu(ga_ref[...])).astype(BF16), wao_ref[...], preferred_element_type=F32)
    y_r = jnp.dot((rnn_ref[...] * jax.nn.silu(gr_ref[...])).astype(BF16), wro_ref[...], preferred_element_type=F32)
    z = jax.nn.sigmoid(ma_ref[...]) * y_a + jax.nn.sigmoid(mr_ref[...]) * y_r
    y_ref[...] = x_ref[...] + jnp.dot(z.astype(BF16), wo_ref[...], preferred_element_type=F32)


def _merge(x, att, ga, rnn, gr, ma, mr, wao_b, wro_b, wo_b):
    rows = x.shape[0]
    tm = min(rows, 256)
    assert rows % tm == 0

    def rowspec(width):
        return pl.BlockSpec((tm, width), lambda i: (i, 0))

    def const(shape):
        return pl.BlockSpec(shape, lambda i: (0,) * len(shape))

    return pl.pallas_call(
        _merge_kernel,
        grid=(rows // tm,),
        in_specs=[rowspec(D_MODEL), rowspec(D_ATT), rowspec(D_ATT), rowspec(D_RNN), rowspec(D_RNN),
                  rowspec(D_MODEL), rowspec(D_MODEL),
                  const((D_ATT, D_MODEL)), const((D_RNN, D_MODEL)), const((D_MODEL, D_MODEL))],
        out_specs=rowspec(D_MODEL),
        out_shape=jax.ShapeDtypeStruct((rows, D_MODEL), F32),
        compiler_params=pltpu.CompilerParams(dimension_semantics=("arbitrary",), vmem_limit_bytes=VMEM_LIMIT),
        name="merge",
    )(x, att, ga, rnn, gr, ma, mr, wao_b, wro_b, wo_b)


def kernel(x_prompt, x_sample, cache_k, cache_v, state_conv, state_h, page_table, norm_w, w_in, q_norm_w, k_norm_w,
           conv_w, conv_b, w_gate_a, b_gate_a, w_gate_x, b_gate_x, lru_lambda, w_attn_out, w_rnn_out, w_out):
    batch, seq, _ = x_prompt.shape
    dec_batch, dec_seq, _ = x_sample.shape
    depth = w_in.shape[0]
    n_pool = cache_k.shape[1]
    past_len = page_table.shape[1] * PAGE_SIZE
    n_blocks = seq // MOBA_BLOCK

    slopes = jnp.asarray([2.0 ** (-8.0 * (h + 1) / N_HEADS) for h in range(N_HEADS)], F32)
    head_of_lane = np.arange(D_ATT) // HEAD_DIM
    gmean = jnp.asarray((head_of_lane[:, None] == head_of_lane[None, :]) / HEAD_DIM, BF16)
    n_rows = dec_seq * N_HEADS
    row_head = np.arange(SAMPLE_ROWS) % N_HEADS
    row_live = np.arange(SAMPLE_ROWS) < n_rows
    slope_rows = jnp.asarray(np.where(row_live, 2.0 ** (-8.0 * (row_head + 1) / N_HEADS), 0.0), F32)[:, None]
    tpos_rows = jnp.asarray(np.where(row_live, np.arange(SAMPLE_ROWS) // N_HEADS, 0), F32)[:, None]
    hmask = jnp.asarray((row_head[:, None] == head_of_lane[None, :]) & row_live[:, None], F32)

    w_in_b = w_in.astype(BF16)
    wa_b = w_gate_a.astype(BF16)
    wx_b = w_gate_x.astype(BF16)
    wao_b = w_attn_out.astype(BF16)
    wro_b = w_rnn_out.astype(BF16)
    wo_b = w_out.astype(BF16)
    cache_kt = jnp.transpose(cache_k, (0, 1, 3, 4, 2)).reshape(depth, n_pool, D_ATT, PAGE_SIZE)
    cache_vt = jnp.transpose(cache_v, (0, 1, 3, 4, 2)).reshape(depth, n_pool, D_ATT, PAGE_SIZE)

    y_p = x_prompt.reshape(batch * seq, D_MODEL)
    y_s = x_sample.reshape(dec_batch * dec_seq, D_MODEL)
    outs = {name: [] for name in ("kp", "vp", "cp", "hp", "ks", "vs", "cs", "hs")}
    for l in range(depth):
        nw = norm_w[l][None, :]
        qn_t = jnp.tile(q_norm_w[l], N_HEADS)[None, :]
        kn_t = jnp.tile(k_norm_w[l], N_HEADS)[None, :]
        rnn_w = (conv_w[l], conv_b[l][None, :], wa_b[l], b_gate_a[l][None, :], wx_b[l], b_gate_x[l][None, :],
                 lru_lambda[l][None, :])
        out_w = (wao_b[l], wro_b[l], wo_b[l])

        q, kt, vt, ka, va, kmean, ga, xr, gr, ma, mr = _proj_prompt(y_p, batch, nw, w_in_b[l], qn_t, kn_t, gmean)
        att = _moba_prompt(slopes, q.reshape(batch, seq, D_ATT),
                           ka.reshape(batch, n_blocks, N_HEADS, 2 * HEAD_DIM, MOBA_BLOCK), va,
                           kmean.reshape(batch, n_blocks, D_ATT))
        rnn, c_new, h_new = _rglru_prompt(xr.reshape(batch, seq, D_RNN), *rnn_w)
        y_p = _merge(y_p, att.reshape(batch * seq, D_ATT), ga, rnn.reshape(batch * seq, D_RNN), gr, ma, mr, *out_w)
        outs["kp"].append(kt)
        outs["vp"].append(vt)
        outs["cp"].append(c_new)
        outs["hp"].append(h_new.reshape(batch, D_RNN))

        q, k, v, ga, xr, gr, ma, mr = _proj_sample(y_s, nw, w_in_b[l], qn_t, kn_t, gmean)
        att = _moba_sample(l, page_table, q.reshape(dec_batch, dec_seq, D_ATT), k.reshape(dec_batch, dec_seq, D_ATT),
                           jnp.swapaxes(v.reshape(dec_batch, dec_seq, D_ATT), 1, 2), cache_kt, cache_vt,
                           slope_rows, tpos_rows, hmask, past_len)
        xr_t = jnp.swapaxes(xr.reshape(dec_batch, dec_seq, D_RNN), 0, 1)
        prev_t = jnp.swapaxes(state_conv[l], 0, 1)
        rnn_t, h_new = _rglru_sample(xr_t, prev_t, state_h[l], *rnn_w)
        rnn = jnp.swapaxes(rnn_t, 0, 1).reshape(dec_batch * dec_seq, D_RNN)
        c_new = jnp.swapaxes(jnp.concatenate([prev_t, xr_t], axis=0)[dec_seq:], 0, 1)
        y_s = _merge(y_s, att.reshape(dec_batch * dec_seq, D_ATT), ga, rnn, gr, ma, mr, *out_w)
        outs["ks"].append(k.reshape(dec_batch, dec_seq, N_HEADS, HEAD_DIM))
        outs["vs"].append(v.reshape(dec_batch, dec_seq, N_HEADS, HEAD_DIM))
        outs["cs"].append(c_new)
        outs["hs"].append(h_new)

    def from_transposed(parts):
        t = jnp.stack(parts).reshape(depth, batch, N_HEADS, HEAD_DIM, seq)
        return jnp.transpose(t, (0, 1, 4, 2, 3))

    return (y_p.reshape(batch, seq, D_MODEL), y_s.reshape(dec_batch, dec_seq, D_MODEL),
            from_transposed(outs["kp"]), from_transposed(outs["vp"]), jnp.stack(outs["cp"]), jnp.stack(outs["hp"]),
            jnp.stack(outs["ks"]), jnp.stack(outs["vs"]), jnp.stack(outs["cs"]), jnp.stack(outs["hs"]))
```

```python
import functools
import math

import numpy as np
import jax
import jax.numpy as jnp
from jax import lax
from jax.experimental import pallas as pl
from jax.experimental.pallas import tpu as pltpu

F32 = jnp.float32
BF16 = jnp.bfloat16

D_MODEL = 1024
N_HEADS = 8
HEAD_DIM = 64
D_ATT = N_HEADS * HEAD_DIM
D_RNN = D_MODEL
N_RNN_BLOCKS = 8
RNN_BLOCK = D_RNN // N_RNN_BLOCKS
CONV_W = 4
LRU_C = 8.0
MOBA_BLOCK = 256
MOBA_TOPK = 3
PAGE_SIZE = 128
RMS_EPS = 1e-6
SPLITS = (D_ATT, D_ATT, D_ATT, D_ATT, D_RNN, D_RNN, D_MODEL, D_MODEL)
N_IN = sum(SPLITS)
CUTS = tuple(int(c) for c in np.cumsum((0,) + SPLITS))

LANES = 128
HEADS_PER_SLAB = LANES // HEAD_DIM
N_SLABS = D_ATT // LANES
MASK_BIAS = -1e30
VMEM_LIMIT = 56 * 1024 * 1024

PAGES_PER_STEP = 16
BLOCKS_PER_STEP = PAGES_PER_STEP * PAGE_SIZE // MOBA_BLOCK
PAGES_PER_BLOCK = MOBA_BLOCK // PAGE_SIZE
SAMPLE_ROWS = LANES

AUG_ROW_ONES_I = 0
AUG_ROW_ONES_QB = 1
AUG_ROW_KEY = 2
AUG_ROW_BLOCK = 3
AUG_ROWS = 4
ROW_CHUNK = 64
HEADS_PER_STEP = 4
assert all(math.frexp(2.0 ** (-8.0 * (h + 1) / N_HEADS))[0] == 0.5 for h in range(N_HEADS)) and MOBA_BLOCK <= 256


def _hilo(x):
    hi = x.astype(BF16)
    lo = (x - hi.astype(F32)).astype(BF16)
    return hi, lo


def _dot_nt(a, b):
    return lax.dot_general(a, b, (((1,), (1,)), ((), ())), preferred_element_type=F32)


def _top_blocks(gate, col, n_sel):
    sel = jnp.zeros(gate.shape, jnp.bool_)
    colf = col.astype(F32)
    for _ in range(n_sel):
        m = jnp.max(gate, axis=-1, keepdims=True)
        idx = jnp.min(jnp.where(gate == m, colf, float(gate.shape[-1])), axis=-1, keepdims=True)
        hit = colf == idx
        sel = jnp.logical_or(sel, jnp.logical_and(hit, m > -jnp.inf))
        gate = jnp.where(hit, -jnp.inf, gate)
    return sel


def _proj_parts(x_ref, nw_ref, w_ref, qn_ref, kn_ref, gmean_ref):
    x = x_ref[...]
    ms = jnp.mean(x * x, axis=-1, keepdims=True)
    h = (x * lax.rsqrt(ms + RMS_EPS) * nw_ref[...]).astype(BF16)

    def part(i):
        return jnp.dot(h, w_ref[:, CUTS[i]:CUTS[i + 1]], preferred_element_type=F32)

    def head_norm(z, w):
        sq_hi, sq_lo = _hilo(z * z)
        g = gmean_ref[...]
        ms_h = jnp.dot(sq_hi, g, preferred_element_type=F32) + jnp.dot(sq_lo, g, preferred_element_type=F32)
        return z * lax.rsqrt(ms_h + RMS_EPS) * w

    q = head_norm(part(0), qn_ref[...]) * (HEAD_DIM ** -0.5)
    k = head_norm(part(1), kn_ref[...])
    return q, k, part


def _proj_sample_kernel(x_ref, nw_ref, w_ref, qn_ref, kn_ref, gmean_ref,
                        q_ref, k_ref, v_ref, ga_ref, xr_ref, gr_ref, ma_ref, mr_ref):
    q, k, part = _proj_parts(x_ref, nw_ref, w_ref, qn_ref, kn_ref, gmean_ref)
    q_ref[...] = q
    k_ref[...] = k
    for i, ref in zip(range(2, 8), (v_ref, ga_ref, xr_ref, gr_ref, ma_ref, mr_ref)):
        ref[...] = part(i)


def _proj_prompt_kernel(n_blocks, x_ref, nw_ref, w_ref, qn_ref, kn_ref, gmean_ref,
                        q_ref, kt_ref, vt_ref, ka_ref, va_ref, km_ref, ga_ref, xr_ref, gr_ref, ma_ref, mr_ref):
    q, k, part = _proj_parts(x_ref, nw_ref, w_ref, qn_ref, kn_ref, gmean_ref)
    v = part(2)
    q_ref[...] = q
    for i, ref in zip(range(3, 8), (ga_ref, xr_ref, gr_ref, ma_ref, mr_ref)):
        ref[...] = part(i)
    km_ref[...] = jnp.mean(k, axis=0, keepdims=True)
    kt = jnp.transpose(k)
    kt_ref[...] = kt
    vt_ref[...] = jnp.transpose(v)
    n = pl.program_id(0) % n_blocks
    r = lax.broadcasted_iota(jnp.int32, (HEAD_DIM, MOBA_BLOCK), 0)
    j = lax.broadcasted_iota(jnp.int32, (HEAD_DIM, MOBA_BLOCK), 1).astype(F32)
    ones_rows = (r == n) | (r == n_blocks + AUG_ROW_ONES_I) | (r == n_blocks + AUG_ROW_ONES_QB)
    lane = lax.broadcasted_iota(jnp.int32, (MOBA_BLOCK, LANES), 1)
    for h in range(N_HEADS):
        slope = 2.0 ** (-8.0 * (h + 1) / N_HEADS)
        extra = jnp.where(ones_rows, 1.0, 0.0)
        extra = jnp.where(r == n_blocks + AUG_ROW_KEY, slope * j, extra)
        extra = jnp.where(r == n_blocks + AUG_ROW_BLOCK, slope * (n * MOBA_BLOCK).astype(F32), extra)
        ka_ref[h] = jnp.concatenate([kt[h * HEAD_DIM:(h + 1) * HEAD_DIM].astype(BF16), extra.astype(BF16)], axis=0)
        slab = v[:, (h // HEADS_PER_SLAB) * LANES:(h // HEADS_PER_SLAB + 1) * LANES]
        own = (lane // HEAD_DIM) == (h % HEADS_PER_SLAB)
        va_ref[h] = jnp.where(own, slab, 1.0).astype(BF16)


def _proj_in_specs(tm):
    def const(shape):
        return pl.BlockSpec(shape, lambda i: (0,) * len(shape))

    return [pl.BlockSpec((tm, D_MODEL), lambda i: (i, 0)), const((1, D_MODEL)), const((D_MODEL, N_IN)),
            const((1, D_ATT)), const((1, D_ATT)), const((D_ATT, D_ATT))]


def _proj_sample(x, norm_w, w_in_b, qn_t, kn_t, gmean):
    rows = x.shape[0]
    widths = (D_ATT, D_ATT, D_ATT, D_ATT, D_RNN, D_RNN, D_MODEL, D_MODEL)
    return pl.pallas_call(
        _proj_sample_kernel,
        grid=(1,),
        in_specs=_proj_in_specs(rows),
        out_specs=[pl.BlockSpec((rows, w), lambda i: (i, 0)) for w in widths],
        out_shape=[jax.ShapeDtypeStruct((rows, w), F32) for w in widths],
        compiler_params=pltpu.CompilerParams(dimension_semantics=("arbitrary",), vmem_limit_bytes=VMEM_LIMIT),
        name="proj_sample",
    )(x, norm_w, w_in_b, qn_t, kn_t, gmean)


def _proj_prompt(x, batch, norm_w, w_in_b, qn_t, kn_t, gmean):
    rows = x.shape[0]
    seq = rows // batch
    tm = MOBA_BLOCK
    n_blocks = seq // tm
    assert seq % tm == 0 and n_blocks + AUG_ROWS <= HEAD_DIM

    def rowspec(width):
        return pl.BlockSpec((tm, width), lambda i: (i, 0))

    def tspec():
        return pl.BlockSpec((None, D_ATT, tm), lambda i: (i // n_blocks, 0, i % n_blocks))

    widths = (D_ATT, D_RNN, D_RNN, D_MODEL, D_MODEL)
    out_specs = [rowspec(D_ATT), tspec(), tspec(),
                 pl.BlockSpec((None, N_HEADS, 2 * HEAD_DIM, tm), lambda i: (i, 0, 0, 0)),
                 pl.BlockSpec((None, N_HEADS, tm, LANES), lambda i: (i // n_blocks, 0, i % n_blocks, 0)),
                 pl.BlockSpec((None, 1, D_ATT), lambda i: (i, 0, 0))] + [rowspec(w) for w in widths]
    out_shape = [jax.ShapeDtypeStruct((rows, D_ATT), F32),
                 jax.ShapeDtypeStruct((batch, D_ATT, seq), F32),
                 jax.ShapeDtypeStruct((batch, D_ATT, seq), F32),
                 jax.ShapeDtypeStruct((batch * n_blocks, N_HEADS, 2 * HEAD_DIM, tm), BF16),
                 jax.ShapeDtypeStruct((batch, N_HEADS, seq, LANES), BF16),
                 jax.ShapeDtypeStruct((batch * n_blocks, 1, D_ATT), F32)] + [
                     jax.ShapeDtypeStruct((rows, w), F32) for w in widths]
    return pl.pallas_call(
        functools.partial(_proj_prompt_kernel, n_blocks),
        grid=(rows // tm,),
        in_specs=_proj_in_specs(tm),
        out_specs=out_specs,
        out_shape=out_shape,
        compiler_params=pltpu.CompilerParams(dimension_semantics=("arbitrary",), vmem_limit_bytes=VMEM_LIMIT),
        name="proj_prompt",
    )(x, norm_w, w_in_b, qn_t, kn_t, gmean)


def _top_blocks_t(gate, n_sel):
    sel = jnp.zeros(gate.shape, jnp.bool_)
    blk = lax.broadcasted_iota(jnp.int32, gate.shape, 0).astype(F32)
    for _ in range(n_sel):
        m = jnp.max(gate, axis=0, keepdims=True)
        idx = jnp.min(jnp.where(gate == m, blk, float(gate.shape[0])), axis=0, keepdims=True)
        hit = blk == idx
        sel = jnp.logical_or(sel, jnp.logical_and(hit, m > -jnp.inf))
        gate = jnp.where(hit, -jnp.inf, gate)
    return sel


def _moba_prompt_kernel(slopes_ref, q_ref, ka_ref, va_ref, km_ref, o_ref,
                        qa_past_ref, qa_own_ref, s_ref, p_ref, m_ref, acc_ref):
    qb = pl.program_id(2)
    n_blocks = km_ref.shape[0]
    tq = MOBA_BLOCK
    lane = lax.broadcasted_iota(jnp.int32, (tq, LANES), 1)
    lane_km = lax.broadcasted_iota(jnp.int32, (n_blocks, HEADS_PER_STEP * HEAD_DIM), 1)
    blk = lax.broadcasted_iota(jnp.int32, (n_blocks, tq), 0)
    xrow = lax.broadcasted_iota(jnp.int32, (HEAD_DIM - n_blocks, tq), 0)
    qpos = lax.broadcasted_iota(jnp.int32, (HEAD_DIM - n_blocks, tq), 1).astype(F32)
    q = q_ref[...]
    q_hi, q_lo = _hilo(q)
    km = km_ref[...]
    qb_f = (qb * tq).astype(F32)

    for hh in range(HEADS_PER_STEP):
        slope = slopes_ref[pl.program_id(1) * HEADS_PER_STEP + hh]
        km_hi, km_lo = _hilo(jnp.where((lane_km // HEAD_DIM) == hh, km, 0.0))
        gate = _dot_nt(km_hi, q_hi) + _dot_nt(km_hi, q_lo) + _dot_nt(km_lo, q_hi)
        gate = jnp.where(blk < qb, gate, -jnp.inf)
        sel = _top_blocks_t(gate, min(MOBA_TOPK, n_blocks - 1))
        bias_t = jnp.where(sel, 0.0, MASK_BIAS)
        extra_t = jnp.where(xrow == AUG_ROW_ONES_I, -slope * qpos, 0.0)
        extra_t = jnp.where(xrow == AUG_ROW_ONES_QB, -slope * qb_f, extra_t)
        extra_t = jnp.where((xrow == AUG_ROW_KEY) | (xrow == AUG_ROW_BLOCK), 1.0, extra_t)
        cols = jnp.transpose(jnp.concatenate([jnp.zeros((HEAD_DIM, tq), F32), bias_t, extra_t], axis=0))
        q_pair = q[:, (hh // 2) * LANES:(hh // 2 + 1) * LANES]
        q_low = q_pair if hh % 2 == 0 else pltpu.roll(q_pair, HEAD_DIM, 1)
        qa = jnp.where(lane < HEAD_DIM, q_low, cols)
        qa_past_ref[hh] = qa.astype(BF16)
        qa_own_ref[hh] = jnp.where(lane == HEAD_DIM + qb, 0.0, qa).astype(BF16)
        m_ref[hh] = jnp.full((tq, LANES), -jnp.inf, F32)
        acc_ref[hh] = jnp.zeros((tq, LANES), F32)

    def update(qa_ref, starts, n_blk, causal):
        tiles = n_blk * tq // LANES

        def scores(n0, hh):
            for b in range(n_blk):
                s_ref[hh, :, b * tq:(b + 1) * tq] = jnp.dot(qa_ref[hh], ka_ref[n0 + b, hh],
                                                            preferred_element_type=F32)

        def softmax(n0, hh):
            for c in range(tq // ROW_CHUNK):
                rows = slice(c * ROW_CHUNK, (c + 1) * ROW_CHUNK)
                m_old = m_ref[hh, rows, :]
                s = [s_ref[hh, rows, t * LANES:(t + 1) * LANES] for t in range(tiles)]
                if causal:
                    key = lax.broadcasted_iota(jnp.int32, (ROW_CHUNK, LANES), 1)
                    qry = lax.broadcasted_iota(jnp.int32, (ROW_CHUNK, LANES), 0) + c * ROW_CHUNK
                    s = [jnp.where(key + t * LANES <= qry, s[t], -jnp.inf) for t in range(tiles)]
                m_new = jnp.maximum(m_old, jnp.max(functools.reduce(jnp.maximum, s), axis=-1, keepdims=True))
                for t in range(tiles):
                    p_ref[hh, rows, t * LANES:(t + 1) * LANES] = jnp.exp(s[t] - m_new).astype(BF16)
                acc_ref[hh, rows, :] = acc_ref[hh, rows, :] * jnp.exp(m_old - m_new)
                m_ref[hh, rows, :] = m_new

        def values(n0, hh):
            off = pl.multiple_of(n0 * tq, tq)
            acc_ref[hh] += jnp.dot(p_ref[hh, :, :n_blk * tq], va_ref[hh, pl.ds(off, n_blk * tq), :],
                                   preferred_element_type=F32)

        units = [(n0, hh) for n0 in starts for hh in range(HEADS_PER_STEP)]
        scores(*units[0])
        for u, unit in enumerate(units):
            if u + 1 < len(units):
                scores(*units[u + 1])
            softmax(*unit)
            values(*unit)

    update(qa_own_ref, [qb], 1, True)

    n_pairs = (qb + 1) // 2

    def body(i, carry):
        update(qa_past_ref, [4 * i, 4 * i + 2], 2, False)
        return carry

    lax.fori_loop(0, n_pairs // 2, body, 0)

    @pl.when(n_pairs % 2 == 1)
    def _():
        update(qa_past_ref, [2 * (n_pairs - 1)], 2, False)

    for hh in range(HEADS_PER_STEP):
        acc = acc_ref[hh]
        o_h = acc / pltpu.roll(acc, HEAD_DIM, 1)
        if hh % 2 == 1:
            pair = jnp.where(lane < HEAD_DIM, o_prev, o_h)
            o_ref[:, (hh // 2) * LANES:(hh // 2 + 1) * LANES] = pair
        o_prev = o_h


def _moba_prompt(slopes, q, ka, va, kmean):
    batch, seq, _ = q.shape
    n_blocks = seq // MOBA_BLOCK
    assert seq % MOBA_BLOCK == 0 and n_blocks > MOBA_TOPK and n_blocks % 2 == 0
    assert HEADS_PER_SLAB == 2 and HEADS_PER_STEP % 2 == 0 and N_HEADS % HEADS_PER_STEP == 0
    tq = MOBA_BLOCK
    width = HEADS_PER_STEP * HEAD_DIM
    return pl.pallas_call(
        _moba_prompt_kernel,
        grid=(batch, N_HEADS // HEADS_PER_STEP, n_blocks),
        in_specs=[pl.BlockSpec(memory_space=pltpu.SMEM),
                  pl.BlockSpec((None, tq, width), lambda b, g, i: (b, i, g)),
                  pl.BlockSpec((None, n_blocks, HEADS_PER_STEP, 2 * HEAD_DIM, tq), lambda b, g, i: (b, 0, g, 0, 0)),
                  pl.BlockSpec((None, HEADS_PER_STEP, seq, LANES), lambda b, g, i: (b, g, 0, 0)),
                  pl.BlockSpec((None, n_blocks, width), lambda b, g, i: (b, 0, g))],
        out_specs=pl.BlockSpec((None, tq, width), lambda b, g, i: (b, i, g)),
        out_shape=jax.ShapeDtypeStruct((batch, seq, D_ATT), F32),
        scratch_shapes=[pltpu.VMEM((HEADS_PER_STEP, tq, 2 * HEAD_DIM), BF16),
                        pltpu.VMEM((HEADS_PER_STEP, tq, 2 * HEAD_DIM), BF16),
                        pltpu.VMEM((HEADS_PER_STEP, tq, 2 * tq), F32),
                        pltpu.VMEM((HEADS_PER_STEP, tq, 2 * tq), BF16),
                        pltpu.VMEM((HEADS_PER_STEP, tq, LANES), F32),
                        pltpu.VMEM((HEADS_PER_STEP, tq, LANES), F32)],
        compiler_params=pltpu.CompilerParams(dimension_semantics=("arbitrary",) * 3, vmem_limit_bytes=VMEM_LIMIT),
        name="moba_prompt",
    )(slopes, q, ka, va, kmean)


def _moba_sample_kernel(n_past_blocks, past_len, pt_ref, q_ref, kn_ref, vnt_ref, slope_ref, tpos_ref, hmask_ref,
                        *rest):
    k_pages = rest[:PAGES_PER_STEP]
    v_pages = rest[PAGES_PER_STEP:2 * PAGES_PER_STEP]
    o_ref, o_all, m_all, l_all, g_all, s_ref, p_ref = rest[2 * PAGES_PER_STEP:]
    del pt_ref
    j = pl.program_id(1)
    n_tok = q_ref.shape[0]
    n_rows = n_tok * N_HEADS
    hmask = hmask_ref[...]
    slope = slope_ref[...]
    tpos = tpos_ref[...]
    q = q_ref[...]
    qbd = jnp.concatenate([jnp.broadcast_to(q[t:t + 1, :], (N_HEADS, D_ATT)) for t in range(n_tok)], axis=0) * hmask
    q_hilo = jnp.concatenate(_hilo(qbd), axis=0)
    colk = lax.broadcasted_iota(jnp.int32, (n_rows, MOBA_BLOCK), 1).astype(F32)
    colb = lax.broadcasted_iota(jnp.int32, (n_rows, LANES), 1)

    @pl.when(j == 0)
    def _():
        m_all[...] = jnp.full(m_all.shape, -jnp.inf, F32)
        l_all[...] = jnp.zeros(l_all.shape, F32)
        g_all[...] = jnp.full(g_all.shape, -jnp.inf, F32)
        p_ref[...] = jnp.zeros(p_ref.shape, BF16)

    def block_t(page_refs, i):
        pages = range(i * PAGES_PER_BLOCK, (i + 1) * PAGES_PER_BLOCK)
        return jnp.concatenate([page_refs[p][...] for p in pages], axis=1).astype(BF16)

    def scores(i):
        kt = block_t(k_pages, i)
        both = jnp.dot(q_hilo, kt, preferred_element_type=F32)
        s_ref[i] = both[:n_rows] + both[n_rows:]

    def softmax(i):
        n = j * BLOCKS_PER_STEP + i
        raw = s_ref[i]
        gate = jnp.mean(raw, axis=-1, keepdims=True)
        dist = (past_len + tpos - (n * MOBA_BLOCK).astype(F32)) - colk
        s = raw - slope * dist
        m = jnp.max(s, axis=-1, keepdims=True)
        p = jnp.exp(s - m)
        l = jnp.sum(p, axis=-1, keepdims=True)
        p_ref[i, :n_rows, :] = p.astype(BF16)
        here = colb == n
        m_all[...] = jnp.where(here, m, m_all[...])
        l_all[...] = jnp.where(here, l, l_all[...])
        g_all[...] = jnp.where(here, gate, g_all[...])

    def values(i):
        o_all[j * BLOCKS_PER_STEP + i] = _dot_nt(block_t(v_pages, i), p_ref[i])

    scores(0)
    for i in range(BLOCKS_PER_STEP):
        if i + 1 < BLOCKS_PER_STEP:
            scores(i + 1)
        softmax(i)
        values(i)

    @pl.when(j == pl.num_programs(1) - 1)
    def _():
        sel = _top_blocks(g_all[...], colb, min(MOBA_TOPK, n_past_blocks))
        kn = kn_ref[...]
        s_own = []
        for t in range(n_tok):
            s_t = jnp.sum(qbd * kn[t:t + 1, :], axis=-1, keepdims=True) - slope * (tpos - float(t))
            s_own.append(jnp.where(tpos >= float(t), s_t, -jnp.inf))
        m_own = functools.reduce(jnp.maximum, s_own)
        m_sel = jnp.max(jnp.where(sel, m_all[...], -jnp.inf), axis=-1, keepdims=True)
        m_tot = jnp.maximum(m_own, m_sel)
        w = jnp.where(sel, jnp.exp(m_all[...] - m_tot), 0.0)
        l_tot = jnp.sum(w * l_all[...], axis=-1, keepdims=True)
        for t in range(n_tok):
            p_t = jnp.exp(s_own[t] - m_tot)
            l_tot = l_tot + p_t
            w = jnp.where(colb == n_past_blocks + t, p_t, w)
        w = jnp.concatenate([w / l_tot, jnp.zeros((SAMPLE_ROWS - n_rows, LANES), F32)], axis=0)
        wt = jnp.transpose(w)
        acc = jnp.zeros((D_ATT, SAMPLE_ROWS), F32)
        for n in range(n_past_blocks):
            acc = acc + o_all[n] * wt[n:n + 1, :]
        vnt = vnt_ref[...]
        for t in range(n_tok):
            acc = acc + vnt[:, t:t + 1] * wt[n_past_blocks + t:n_past_blocks + t + 1, :]
        out = jnp.transpose(acc)[:n_rows] * hmask
        o_ref[...] = jnp.sum(out.reshape(n_tok, N_HEADS, D_ATT), axis=1)


def _moba_sample(layer, page_table, q, k_new, v_new_t, cache_kt, cache_vt, slope_rows, tpos_rows, hmask, past_len):
    dec_batch, n_tok, _ = q.shape
    n_pages = page_table.shape[1]
    assert past_len % MOBA_BLOCK == 0 and n_pages * PAGE_SIZE == past_len and n_pages % PAGES_PER_STEP == 0
    n_past_blocks = past_len // MOBA_BLOCK
    n_rows = n_tok * N_HEADS
    assert SAMPLE_ROWS == LANES and n_rows <= SAMPLE_ROWS and 0 < n_past_blocks <= LANES - n_tok
    assert n_rows % 16 == 0
    n_steps = n_pages // PAGES_PER_STEP

    def tokspec():
        return pl.BlockSpec((None, n_tok, D_ATT), lambda b, j, pt: (b, 0, 0))

    def const(shape):
        return pl.BlockSpec(shape, lambda b, j, pt: (0,) * len(shape))

    def pagespec(i):
        return pl.BlockSpec((None, None, D_ATT, PAGE_SIZE),
                            lambda b, j, pt: (layer, pt[b * n_pages + j * PAGES_PER_STEP + i], 0, 0))

    grid_spec = pltpu.PrefetchScalarGridSpec(
        num_scalar_prefetch=1,
        grid=(dec_batch, n_steps),
        in_specs=[tokspec(), tokspec(), pl.BlockSpec((None, D_ATT, n_tok), lambda b, j, pt: (b, 0, 0)),
                  const((n_rows, 1)), const((n_rows, 1)), const((n_rows, D_ATT))]
        + [pagespec(i) for i in range(PAGES_PER_STEP)] * 2,
        out_specs=tokspec(),
        scratch_shapes=[pltpu.VMEM((n_past_blocks, D_ATT, SAMPLE_ROWS), F32),
                        pltpu.VMEM((n_rows, LANES), F32),
                        pltpu.VMEM((n_rows, LANES), F32),
                        pltpu.VMEM((n_rows, LANES), F32),
                        pltpu.VMEM((BLOCKS_PER_STEP, n_rows, MOBA_BLOCK), F32),
                        pltpu.VMEM((BLOCKS_PER_STEP, SAMPLE_ROWS, MOBA_BLOCK), BF16)],
    )
    return pl.pallas_call(
        functools.partial(_moba_sample_kernel, n_past_blocks, float(past_len)),
        grid_spec=grid_spec,
        out_shape=jax.ShapeDtypeStruct((dec_batch, n_tok, D_ATT), F32),
        compiler_params=pltpu.CompilerParams(dimension_semantics=("arbitrary",) * 2, vmem_limit_bytes=VMEM_LIMIT),
        name="moba_sample",
    )(page_table.reshape(-1), q, k_new, v_new_t, slope_rows, tpos_rows, hmask,
      *([cache_kt] * PAGES_PER_STEP), *([cache_vt] * PAGES_PER_STEP))


def _rglru_coeffs(xc, wa_ref, ba_ref, wx_ref, bx_ref, lam_ref):
    xb = xc.astype(BF16)

    def block_diag(w_ref):
        return jnp.concatenate(
            [jnp.dot(xb[:, n * RNN_BLOCK:(n + 1) * RNN_BLOCK], w_ref[n], preferred_element_type=F32)
             for n in range(N_RNN_BLOCKS)], axis=-1)

    r = jax.nn.sigmoid(block_diag(wa_ref) + ba_ref[...])
    gate_i = jax.nn.sigmoid(block_diag(wx_ref) + bx_ref[...])
    z = -lam_ref[...]
    softplus = jnp.maximum(z, 0.0) + jnp.log1p(jnp.exp(-jnp.abs(z)))
    log_a = -LRU_C * r * softplus
    a = jnp.exp(log_a)
    b = jnp.sqrt(1.0 - a * a) * (gate_i * xc)
    return a, b


def _rglru_prompt_kernel(x_ref, cw_ref, cb_ref, wa_ref, ba_ref, wx_ref, bx_ref, lam_ref,
                         y_ref, conv_ref, hlast_ref, xtail, hcarry):
    t = pl.program_id(1)
    tt = x_ref.shape[0]

    @pl.when(t == 0)
    def _():
        xtail[...] = jnp.zeros(xtail.shape, F32)
        hcarry[...] = jnp.zeros(hcarry.shape, F32)

    x = x_ref[...]
    tail = xtail[...]
    row = lax.broadcasted_iota(jnp.int32, (tt, D_RNN), 0)
    row8 = lax.broadcasted_iota(jnp.int32, (8, D_RNN), 0)
    xc = cb_ref[...] + x * cw_ref[CONV_W - 1:CONV_W, :]
    for d in range(1, CONV_W):
        head = jnp.where(row8 < d, pltpu.roll(tail, d, 0), pltpu.roll(x[:8], d, 0))
        xs = jnp.concatenate([head, pltpu.roll(x, d, 0)[8:]], axis=0)
        xc = xc + xs * cw_ref[CONV_W - 1 - d:CONV_W - d, :]
    a, b = _rglru_coeffs(xc, wa_ref, ba_ref, wx_ref, bx_ref, lam_ref)
    d = 1
    while d < tt:
        keep = row >= d
        b = jnp.where(keep, a * pltpu.roll(b, d, 0) + b, b)
        a = jnp.where(keep, a * pltpu.roll(a, d, 0), a)
        d *= 2
    h = a * hcarry[...] + b
    y_ref[...] = h
    xtail[...] = x[tt - 8:]
    hcarry[...] = h[tt - 1:tt]

    @pl.when(t == pl.num_programs(1) - 1)
    def _():
        conv_ref[...] = x_ref[tt - (CONV_W - 1):tt, :]
        hlast_ref[...] = h[tt - 1:tt]


def _rglru_prompt(xr, conv_w, conv_b, wa_b, b_a, wx_b, b_x, lam):
    batch, seq, _ = xr.shape
    tt = 256
    assert seq % tt == 0 and tt >= 8

    def const(shape):
        return pl.BlockSpec(shape, lambda b, t: (0,) * len(shape))

    wspec = const((N_RNN_BLOCKS, RNN_BLOCK, RNN_BLOCK))
    vec = const((1, D_RNN))
    return pl.pallas_call(
        _rglru_prompt_kernel,
        grid=(batch, seq // tt),
        in_specs=[pl.BlockSpec((None, tt, D_RNN), lambda b, t: (b, t, 0)),
                  const((CONV_W, D_RNN)), vec, wspec, vec, wspec, vec, vec],
        out_specs=[pl.BlockSpec((None, tt, D_RNN), lambda b, t: (b, t, 0)),
                   pl.BlockSpec((None, CONV_W - 1, D_RNN), lambda b, t: (b, 0, 0)),
                   pl.BlockSpec((None, 1, D_RNN), lambda b, t: (b, 0, 0))],
        out_shape=[jax.ShapeDtypeStruct((batch, seq, D_RNN), F32),
                   jax.ShapeDtypeStruct((batch, CONV_W - 1, D_RNN), F32),
                   jax.ShapeDtypeStruct((batch, 1, D_RNN), F32)],
        scratch_shapes=[pltpu.VMEM((8, D_RNN), F32), pltpu.VMEM((1, D_RNN), F32)],
        compiler_params=pltpu.CompilerParams(dimension_semantics=("arbitrary",) * 2, vmem_limit_bytes=VMEM_LIMIT),
        name="rglru_prompt",
    )(xr, conv_w, conv_b, wa_b, b_a, wx_b, b_x, lam)


def _rglru_sample_kernel(x_ref, prev_ref, h0_ref, cw_ref, cb_ref, wa_ref, ba_ref, wx_ref, bx_ref, lam_ref,
                         y_ref, hlast_ref):
    n_tok, n_seq = x_ref.shape[0], x_ref.shape[1]
    xp = [prev_ref[i] for i in range(CONV_W - 1)] + [x_ref[i] for i in range(n_tok)]
    xc = [cb_ref[...] + functools.reduce(lambda u, w: u + w, [xp[t + j] * cw_ref[j:j + 1, :] for j in range(CONV_W)])
          for t in range(n_tok)]
    a, b = _rglru_coeffs(jnp.concatenate(xc, axis=0), wa_ref, ba_ref, wx_ref, bx_ref, lam_ref)
    h = h0_ref[...]
    for t in range(n_tok):
        h = a[t * n_seq:(t + 1) * n_seq] * h + b[t * n_seq:(t + 1) * n_seq]
        y_ref[t] = h
    hlast_ref[...] = h


def _rglru_sample(xr_t, prev_t, h0, conv_w, conv_b, wa_b, b_a, wx_b, b_x, lam):
    n_tok, n_seq, _ = xr_t.shape
    assert n_seq % 8 == 0
    return pl.pallas_call(
        _rglru_sample_kernel,
        out_shape=[jax.ShapeDtypeStruct((n_tok, n_seq, D_RNN), F32), jax.ShapeDtypeStruct((n_seq, D_RNN), F32)],
        compiler_params=pltpu.CompilerParams(vmem_limit_bytes=VMEM_LIMIT),
        name="rglru_sample",
    )(xr_t, prev_t, h0, conv_w, conv_b, wa_b, b_a, wx_b, b_x, lam)


def _merge_kernel(x_ref, att_ref, ga_ref, rnn_ref, gr_ref, ma_ref, mr_ref, wao_ref, wro_ref, wo_ref, y_ref):
    y_a = jnp.dot((att_ref[...] * jax.nn.silu(ga_ref[...])).astype(BF16), wao_ref[...], preferred_element_type=F32)
    y_r = jnp.dot((rnn_ref[...] * jax.nn.silu(gr_ref[...])).astype(BF16), wro_ref[...], preferred_element_type=F32)
    z = jax.nn.sigmoid(ma_ref[...]) * y_a + jax.nn.sigmoid(mr_ref[...]) * y_r
    y_ref[...] = x_ref[...] + jnp.dot(z.astype(BF16), wo_ref[...], preferred_element_type=F32)


def _merge(x, att, ga, rnn, gr, ma, mr, wao_b, wro_b, wo_b):
    rows = x.shape[0]
    tm = min(rows, 256)
    assert rows % tm == 0

    def rowspec(width):
        return pl.BlockSpec((tm, width), lambda i: (i, 0))

    def const(shape):
        return pl.BlockSpec(shape, lambda i: (0,) * len(shape))

    return pl.pallas_call(
        _merge_kernel,
        grid=(rows // tm,),
        in_specs=[rowspec(D_MODEL), rowspec(D_ATT), rowspec(D_ATT), rowspec(D_RNN), rowspec(D_RNN),
                  rowspec(D_MODEL), rowspec(D_MODEL),
                  const((D_ATT, D_MODEL)), const((D_RNN, D_MODEL)), const((D_MODEL, D_MODEL))],
        out_specs=rowspec(D_MODEL),
        out_shape=jax.ShapeDtypeStruct((rows, D_MODEL), F32),
        compiler_params=pltpu.CompilerParams(dimension_semantics=("arbitrary",), vmem_limit_bytes=VMEM_LIMIT),
        name="merge",
    )(x, att, ga, rnn, gr, ma, mr, wao_b, wro_b, wo_b)


def kernel(x_prompt, x_sample, cache_k, cache_v, state_conv, state_h, page_table, norm_w, w_in, q_norm_w, k_norm_w,
           conv_w, conv_b, w_gate_a, b_gate_a, w_gate_x, b_gate_x, lru_lambda, w_attn_out, w_rnn_out, w_out):
    batch, seq, _ = x_prompt.shape
    dec_batch, dec_seq, _ = x_sample.shape
    depth = w_in.shape[0]
    n_pool = cache_k.shape[1]
    past_len = page_table.shape[1] * PAGE_SIZE
    n_blocks = seq // MOBA_BLOCK

    slopes = jnp.asarray([2.0 ** (-8.0 * (h + 1) / N_HEADS) for h in range(N_HEADS)], F32)
    head_of_lane = np.arange(D_ATT) // HEAD_DIM
    gmean = jnp.asarray((head_of_lane[:, None] == head_of_lane[None, :]) / HEAD_DIM, BF16)
    row_head = np.arange(dec_seq * N_HEADS) % N_HEADS
    slope_rows = jnp.asarray(2.0 ** (-8.0 * (row_head + 1) / N_HEADS), F32)[:, None]
    tpos_rows = jnp.asarray(np.arange(dec_seq * N_HEADS) // N_HEADS, F32)[:, None]
    hmask = jnp.asarray(row_head[:, None] == head_of_lane[None, :], F32)

    w_in_b = w_in.astype(BF16)
    wa_b = w_gate_a.astype(BF16)
    wx_b = w_gate_x.astype(BF16)
    wao_b = w_attn_out.astype(BF16)
    wro_b = w_rnn_out.astype(BF16)
    wo_b = w_out.astype(BF16)
    cache_kt = jnp.transpose(cache_k, (0, 1, 3, 4, 2)).reshape(depth, n_pool, D_ATT, PAGE_SIZE)
    cache_vt = jnp.transpose(cache_v, (0, 1, 3, 4, 2)).reshape(depth, n_pool, D_ATT, PAGE_SIZE)

    y_p = x_prompt.reshape(batch * seq, D_MODEL)
    y_s = x_sample.reshape(dec_batch * dec_seq, D_MODEL)
    outs = {name: [] for name in ("kp", "vp", "cp", "hp", "ks", "vs", "cs", "hs")}
    for l in range(depth):
        nw = norm_w[l][None, :]
        qn_t = jnp.tile(q_norm_w[l], N_HEADS)[None, :]
        kn_t = jnp.tile(k_norm_w[l], N_HEADS)[None, :]
        rnn_w = (conv_w[l], conv_b[l][None, :], wa_b[l], b_gate_a[l][None, :], wx_b[l], b_gate_x[l][None, :],
                 lru_lambda[l][None, :])
        out_w = (wao_b[l], wro_b[l], wo_b[l])

        q, kt, vt, ka, va, kmean, ga, xr, gr, ma, mr = _proj_prompt(y_p, batch, nw, w_in_b[l], qn_t, kn_t, gmean)
        att = _moba_prompt(slopes, q.reshape(batch, seq, D_ATT),
                           ka.reshape(batch, n_blocks, N_HEADS, 2 * HEAD_DIM, MOBA_BLOCK), va,
                           kmean.reshape(batch, n_blocks, D_ATT))
        rnn, c_new, h_new = _rglru_prompt(xr.reshape(batch, seq, D_RNN), *rnn_w)
        y_p = _merge(y_p, att.reshape(batch * seq, D_ATT), ga, rnn.reshape(batch * seq, D_RNN), gr, ma, mr, *out_w)
        outs["kp"].append(kt)
        outs["vp"].append(vt)
        outs["cp"].append(c_new)
        outs["hp"].append(h_new.reshape(batch, D_RNN))

        q, k, v, ga, xr, gr, ma, mr = _proj_sample(y_s, nw, w_in_b[l], qn_t, kn_t, gmean)
        att = _moba_sample(l, page_table, q.reshape(dec_batch, dec_seq, D_ATT), k.reshape(dec_batch, dec_seq, D_ATT),
                           jnp.swapaxes(v.reshape(dec_batch, dec_seq, D_ATT), 1, 2), cache_kt, cache_vt,
                           slope_rows, tpos_rows, hmask, past_len)
        xr_t = jnp.swapaxes(xr.reshape(dec_batch, dec_seq, D_RNN), 0, 1)
        prev_t = jnp.swapaxes(state_conv[l], 0, 1)
        rnn_t, h_new = _rglru_sample(xr_t, prev_t, state_h[l], *rnn_w)
        rnn = jnp.swapaxes(rnn_t, 0, 1).reshape(dec_batch * dec_seq, D_RNN)
        c_new = jnp.swapaxes(jnp.concatenate([prev_t, xr_t], axis=0)[dec_seq:], 0, 1)
        y_s = _merge(y_s, att.reshape(dec_batch * dec_seq, D_ATT), ga, rnn, gr, ma, mr, *out_w)
        outs["ks"].append(k.reshape(dec_batch, dec_seq, N_HEADS, HEAD_DIM))
        outs["vs"].append(v.reshape(dec_batch, dec_seq, N_HEADS, HEAD_DIM))
        outs["cs"].append(c_new)
        outs["hs"].append(h_new)

    def from_transposed(parts):
        t = jnp.stack(parts).reshape(depth, batch, N_HEADS, HEAD_DIM, seq)
        return jnp.transpose(t, (0, 1, 4, 2, 3))

    return (y_p.reshape(batch, seq, D_MODEL), y_s.reshape(dec_batch, dec_seq, D_MODEL),
            from_transposed(outs["kp"]), from_transposed(outs["vp"]), jnp.stack(outs["cp"]), jnp.stack(outs["hp"]),
            jnp.stack(outs["ks"]), jnp.stack(outs["vs"]), jnp.stack(outs["cs"]), jnp.stack(outs["hs"]))
```

```python
import functools
import math

import numpy as np
import jax
import jax.numpy as jnp
from jax import lax
from jax.experimental import pallas as pl
from jax.experimental.pallas import tpu as pltpu

F32 = jnp.float32
BF16 = jnp.bfloat16

D_MODEL = 1024
N_HEADS = 8
HEAD_DIM = 64
D_ATT = N_HEADS * HEAD_DIM
D_RNN = D_MODEL
N_RNN_BLOCKS = 8
RNN_BLOCK = D_RNN // N_RNN_BLOCKS
CONV_W = 4
LRU_C = 8.0
MOBA_BLOCK = 256
MOBA_TOPK = 3
PAGE_SIZE = 128
RMS_EPS = 1e-6
SPLITS = (D_ATT, D_ATT, D_ATT, D_ATT, D_RNN, D_RNN, D_MODEL, D_MODEL)
N_IN = sum(SPLITS)
CUTS = tuple(int(c) for c in np.cumsum((0,) + SPLITS))

LANES = 128
HEADS_PER_SLAB = LANES // HEAD_DIM
N_SLABS = D_ATT // LANES
MASK_BIAS = -1e30
VMEM_LIMIT = 56 * 1024 * 1024

PAGES_PER_STEP = 16
BLOCKS_PER_STEP = PAGES_PER_STEP * PAGE_SIZE // MOBA_BLOCK
PAGES_PER_BLOCK = MOBA_BLOCK // PAGE_SIZE
SAMPLE_ROWS = LANES

AUG_ROW_ONES_I = 0
AUG_ROW_ONES_QB = 1
AUG_ROW_KEY = 2
AUG_ROW_BLOCK = 3
AUG_ROWS = 4
ROW_CHUNK = 64
HEADS_PER_STEP = 4
SCAN_GROUP = 8
assert all(math.frexp(2.0 ** (-8.0 * (h + 1) / N_HEADS))[0] == 0.5 for h in range(N_HEADS)) and MOBA_BLOCK <= 256


def _hilo(x):
    hi = x.astype(BF16)
    lo = (x - hi.astype(F32)).astype(BF16)
    return hi, lo


def _dot_nt(a, b):
    return lax.dot_general(a, b, (((1,), (1,)), ((), ())), preferred_element_type=F32)


def _top_blocks(gate, col, n_sel):
    sel = jnp.zeros(gate.shape, jnp.bool_)
    colf = col.astype(F32)
    for _ in range(n_sel):
        m = jnp.max(gate, axis=-1, keepdims=True)
        idx = jnp.min(jnp.where(gate == m, colf, float(gate.shape[-1])), axis=-1, keepdims=True)
        hit = colf == idx
        sel = jnp.logical_or(sel, jnp.logical_and(hit, m > -jnp.inf))
        gate = jnp.where(hit, -jnp.inf, gate)
    return sel


def _proj_parts(x_ref, nw_ref, w_ref, qn_ref, kn_ref, gmean_ref):
    x = x_ref[...]
    ms = jnp.mean(x * x, axis=-1, keepdims=True)
    h = (x * lax.rsqrt(ms + RMS_EPS) * nw_ref[...]).astype(BF16)

    def part(i):
        return jnp.dot(h, w_ref[:, CUTS[i]:CUTS[i + 1]], preferred_element_type=F32)

    def head_norm(z, w):
        sq_hi, sq_lo = _hilo(z * z)
        g = gmean_ref[...]
        ms_h = jnp.dot(sq_hi, g, preferred_element_type=F32) + jnp.dot(sq_lo, g, preferred_element_type=F32)
        return z * lax.rsqrt(ms_h + RMS_EPS) * w

    q = head_norm(part(0), qn_ref[...]) * (HEAD_DIM ** -0.5)
    k = head_norm(part(1), kn_ref[...])
    return q, k, part


def _proj_sample_kernel(x_ref, nw_ref, w_ref, qn_ref, kn_ref, gmean_ref,
                        q_ref, k_ref, v_ref, ga_ref, xr_ref, gr_ref, ma_ref, mr_ref):
    q, k, part = _proj_parts(x_ref, nw_ref, w_ref, qn_ref, kn_ref, gmean_ref)
    q_ref[...] = q
    k_ref[...] = k
    for i, ref in zip(range(2, 8), (v_ref, ga_ref, xr_ref, gr_ref, ma_ref, mr_ref)):
        ref[...] = part(i)


def _proj_prompt_kernel(n_blocks, n_aliased, x_ref, nw_ref, w_ref, qn_ref, kn_ref, gmean_ref, *rest):
    q_ref, kt_ref, vt_ref, ka_ref, va_ref, km_ref, ga_ref, xr_ref, gr_ref, ma_ref, mr_ref = rest[n_aliased:]
    q, k, part = _proj_parts(x_ref, nw_ref, w_ref, qn_ref, kn_ref, gmean_ref)
    v = part(2)
    q_ref[...] = q
    for i, ref in zip(range(3, 8), (ga_ref, xr_ref, gr_ref, ma_ref, mr_ref)):
        ref[...] = part(i).astype(ref.dtype)
    km_ref[...] = jnp.mean(k, axis=0, keepdims=True)
    kt = jnp.transpose(k)
    kt_ref[...] = kt
    vt_ref[...] = jnp.transpose(v)
    n = pl.program_id(0) % n_blocks
    r = lax.broadcasted_iota(jnp.int32, (HEAD_DIM, MOBA_BLOCK), 0)
    j = lax.broadcasted_iota(jnp.int32, (HEAD_DIM, MOBA_BLOCK), 1).astype(F32)
    ones_rows = (r == n) | (r == n_blocks + AUG_ROW_ONES_I) | (r == n_blocks + AUG_ROW_ONES_QB)
    lane = lax.broadcasted_iota(jnp.int32, (MOBA_BLOCK, LANES), 1)
    for h in range(N_HEADS):
        slope = 2.0 ** (-8.0 * (h + 1) / N_HEADS)
        extra = jnp.where(ones_rows, 1.0, 0.0)
        extra = jnp.where(r == n_blocks + AUG_ROW_KEY, slope * j, extra)
        extra = jnp.where(r == n_blocks + AUG_ROW_BLOCK, slope * (n * MOBA_BLOCK).astype(F32), extra)
        ka_ref[h] = jnp.concatenate([kt[h * HEAD_DIM:(h + 1) * HEAD_DIM].astype(BF16), extra.astype(BF16)], axis=0)
        slab = v[:, (h // HEADS_PER_SLAB) * LANES:(h // HEADS_PER_SLAB + 1) * LANES]
        own = (lane // HEAD_DIM) == (h % HEADS_PER_SLAB)
        va_ref[h] = jnp.where(own, slab, 1.0).astype(BF16)


def _proj_in_specs(tm):
    def const(shape):
        return pl.BlockSpec(shape, lambda i: (0,) * len(shape))

    return [pl.BlockSpec((tm, D_MODEL), lambda i: (i, 0)), const((1, D_MODEL)), const((D_MODEL, N_IN)),
            const((1, D_ATT)), const((1, D_ATT)), const((D_ATT, D_ATT))]


def _proj_sample(x, norm_w, w_in_b, qn_t, kn_t, gmean):
    rows = x.shape[0]
    widths = (D_ATT, D_ATT, D_ATT, D_ATT, D_RNN, D_RNN, D_MODEL, D_MODEL)
    return pl.pallas_call(
        _proj_sample_kernel,
        grid=(1,),
        in_specs=_proj_in_specs(rows),
        out_specs=[pl.BlockSpec((rows, w), lambda i: (i, 0)) for w in widths],
        out_shape=[jax.ShapeDtypeStruct((rows, w), F32) for w in widths],
        compiler_params=pltpu.CompilerParams(dimension_semantics=("arbitrary",), vmem_limit_bytes=VMEM_LIMIT),
        name="proj_sample",
    )(x, norm_w, w_in_b, qn_t, kn_t, gmean)


def _proj_prompt(x, batch, layer, depth, kv_all, norm_w, w_in_b, qn_t, kn_t, gmean):
    rows = x.shape[0]
    seq = rows // batch
    tm = MOBA_BLOCK
    n_blocks = seq // tm
    assert seq % tm == 0 and n_blocks + AUG_ROWS <= HEAD_DIM
    kv_all = () if kv_all is None else tuple(kv_all)
    n_fixed = len(_proj_in_specs(tm))

    def rowspec(width):
        return pl.BlockSpec((tm, width), lambda i: (i, 0))

    def tspec():
        return pl.BlockSpec((None, None, D_ATT, tm), lambda i: (layer, i // n_blocks, 0, i % n_blocks))

    widths = (D_ATT, D_RNN, D_RNN, D_MODEL, D_MODEL)
    out_specs = [rowspec(D_ATT), tspec(), tspec(),
                 pl.BlockSpec((None, N_HEADS, 2 * HEAD_DIM, tm), lambda i: (i, 0, 0, 0)),
                 pl.BlockSpec((None, N_HEADS, tm, LANES), lambda i: (i // n_blocks, 0, i % n_blocks, 0)),
                 pl.BlockSpec((None, 1, D_ATT), lambda i: (i, 0, 0))] + [rowspec(w) for w in widths]
    out_shape = [jax.ShapeDtypeStruct((rows, D_ATT), F32),
                 jax.ShapeDtypeStruct((depth, batch, D_ATT, seq), F32),
                 jax.ShapeDtypeStruct((depth, batch, D_ATT, seq), F32),
                 jax.ShapeDtypeStruct((batch * n_blocks, N_HEADS, 2 * HEAD_DIM, tm), BF16),
                 jax.ShapeDtypeStruct((batch, N_HEADS, seq, LANES), BF16),
                 jax.ShapeDtypeStruct((batch * n_blocks, 1, D_ATT), F32)] + [
                     jax.ShapeDtypeStruct((rows, w), d) for w, d in zip(widths, (BF16, F32, BF16, BF16, BF16))]
    return pl.pallas_call(
        functools.partial(_proj_prompt_kernel, n_blocks, len(kv_all)),
        grid=(rows // tm,),
        in_specs=_proj_in_specs(tm) + [pl.BlockSpec(memory_space=pl.ANY)] * len(kv_all),
        out_specs=out_specs,
        out_shape=out_shape,
        input_output_aliases={n_fixed + i: 1 + i for i in range(len(kv_all))},
        compiler_params=pltpu.CompilerParams(dimension_semantics=("arbitrary",), vmem_limit_bytes=VMEM_LIMIT),
        name="proj_prompt",
    )(x, norm_w, w_in_b, qn_t, kn_t, gmean, *kv_all)


def _top_blocks_t(gate, n_sel):
    sel = jnp.zeros(gate.shape, jnp.bool_)
    blk = lax.broadcasted_iota(jnp.int32, gate.shape, 0).astype(F32)
    for _ in range(n_sel):
        m = jnp.max(gate, axis=0, keepdims=True)
        idx = jnp.min(jnp.where(gate == m, blk, float(gate.shape[0])), axis=0, keepdims=True)
        hit = blk == idx
        sel = jnp.logical_or(sel, jnp.logical_and(hit, m > -jnp.inf))
        gate = jnp.where(hit, -jnp.inf, gate)
    return sel


def _moba_prompt_kernel(slopes_ref, q_ref, ka_ref, va_ref, km_ref, o_ref,
                        qa_past_ref, qa_own_ref, s_ref, p_ref, m_ref, acc_ref):
    qb = pl.program_id(2)
    n_blocks = km_ref.shape[0]
    tq = MOBA_BLOCK
    lane = lax.broadcasted_iota(jnp.int32, (tq, LANES), 1)
    lane_km = lax.broadcasted_iota(jnp.int32, (n_blocks, HEADS_PER_STEP * HEAD_DIM), 1)
    blk = lax.broadcasted_iota(jnp.int32, (n_blocks, tq), 0)
    xrow = lax.broadcasted_iota(jnp.int32, (HEAD_DIM - n_blocks, tq), 0)
    qpos = lax.broadcasted_iota(jnp.int32, (HEAD_DIM - n_blocks, tq), 1).astype(F32)
    q = q_ref[...]
    q_hi, q_lo = _hilo(q)
    km = km_ref[...]
    qb_f = (qb * tq).astype(F32)

    for hh in range(HEADS_PER_STEP):
        slope = slopes_ref[pl.program_id(1) * HEADS_PER_STEP + hh]
        km_hi, km_lo = _hilo(jnp.where((lane_km // HEAD_DIM) == hh, km, 0.0))
        gate = _dot_nt(km_hi, q_hi) + _dot_nt(km_hi, q_lo) + _dot_nt(km_lo, q_hi)
        gate = jnp.where(blk < qb, gate, -jnp.inf)
        sel = _top_blocks_t(gate, min(MOBA_TOPK, n_blocks - 1))
        bias_t = jnp.where(sel, 0.0, MASK_BIAS)
        extra_t = jnp.where(xrow == AUG_ROW_ONES_I, -slope * qpos, 0.0)
        extra_t = jnp.where(xrow == AUG_ROW_ONES_QB, -slope * qb_f, extra_t)
        extra_t = jnp.where((xrow == AUG_ROW_KEY) | (xrow == AUG_ROW_BLOCK), 1.0, extra_t)
        cols = jnp.transpose(jnp.concatenate([jnp.zeros((HEAD_DIM, tq), F32), bias_t, extra_t], axis=0))
        q_pair = q[:, (hh // 2) * LANES:(hh // 2 + 1) * LANES]
        q_low = q_pair if hh % 2 == 0 else pltpu.roll(q_pair, HEAD_DIM, 1)
        qa = jnp.where(lane < HEAD_DIM, q_low, cols)
        qa_past_ref[hh] = qa.astype(BF16)
        qa_own_ref[hh] = jnp.where(lane == HEAD_DIM + qb, 0.0, qa).astype(BF16)
        m_ref[hh] = jnp.full((tq, LANES), -jnp.inf, F32)
        acc_ref[hh] = jnp.zeros((tq, LANES), F32)

    def update(qa_ref, starts, n_blk, causal):
        tiles = n_blk * tq // LANES

        def scores(n0, hh):
            for b in range(n_blk):
                s_ref[hh, :, b * tq:(b + 1) * tq] = jnp.dot(qa_ref[hh], ka_ref[n0 + b, hh],
                                                            preferred_element_type=F32)

        def softmax(n0, hh):
            for c in range(tq // ROW_CHUNK):
                rows = slice(c * ROW_CHUNK, (c + 1) * ROW_CHUNK)
                m_old = m_ref[hh, rows, :]
                s = [s_ref[hh, rows, t * LANES:(t + 1) * LANES] for t in range(tiles)]
                if causal:
                    key = lax.broadcasted_iota(jnp.int32, (ROW_CHUNK, LANES), 1)
                    qry = lax.broadcasted_iota(jnp.int32, (ROW_CHUNK, LANES), 0) + c * ROW_CHUNK
                    s = [jnp.where(key + t * LANES <= qry, s[t], -jnp.inf) for t in range(tiles)]
                m_new = jnp.maximum(m_old, jnp.max(functools.reduce(jnp.maximum, s), axis=-1, keepdims=True))
                for t in range(tiles):
                    p_ref[hh, rows, t * LANES:(t + 1) * LANES] = jnp.exp(s[t] - m_new).astype(BF16)
                acc_ref[hh, rows, :] = acc_ref[hh, rows, :] * jnp.exp(m_old - m_new)
                m_ref[hh, rows, :] = m_new

        def values(n0, hh):
            off = pl.multiple_of(n0 * tq, tq)
            acc_ref[hh] += jnp.dot(p_ref[hh, :, :n_blk * tq], va_ref[hh, pl.ds(off, n_blk * tq), :],
                                   preferred_element_type=F32)

        units = [(n0, hh) for n0 in starts for hh in range(HEADS_PER_STEP)]
        scores(*units[0])
        for u, unit in enumerate(units):
            if u + 1 < len(units):
                scores(*units[u + 1])
            softmax(*unit)
            values(*unit)

    update(qa_own_ref, [qb], 1, True)

    n_pairs = (qb + 1) // 2

    def body(i, carry):
        update(qa_past_ref, [4 * i, 4 * i + 2], 2, False)
        return carry

    lax.fori_loop(0, n_pairs // 2, body, 0)

    @pl.when(n_pairs % 2 == 1)
    def _():
        update(qa_past_ref, [2 * (n_pairs - 1)], 2, False)

    for hh in range(HEADS_PER_STEP):
        acc = acc_ref[hh]
        o_h = acc / pltpu.roll(acc, HEAD_DIM, 1)
        if hh % 2 == 1:
            pair = jnp.where(lane < HEAD_DIM, o_prev, o_h)
            o_ref[:, (hh // 2) * LANES:(hh // 2 + 1) * LANES] = pair.astype(o_ref.dtype)
        o_prev = o_h


def _moba_prompt(slopes, q, ka, va, kmean):
    batch, seq, _ = q.shape
    n_blocks = seq // MOBA_BLOCK
    assert seq % MOBA_BLOCK == 0 and n_blocks > MOBA_TOPK and n_blocks % 2 == 0
    assert HEADS_PER_SLAB == 2 and HEADS_PER_STEP % 2 == 0 and N_HEADS % HEADS_PER_STEP == 0
    tq = MOBA_BLOCK
    width = HEADS_PER_STEP * HEAD_DIM
    return pl.pallas_call(
        _moba_prompt_kernel,
        grid=(batch, N_HEADS // HEADS_PER_STEP, n_blocks),
        in_specs=[pl.BlockSpec(memory_space=pltpu.SMEM),
                  pl.BlockSpec((None, tq, width), lambda b, g, i: (b, i, g)),
                  pl.BlockSpec((None, n_blocks, HEADS_PER_STEP, 2 * HEAD_DIM, tq), lambda b, g, i: (b, 0, g, 0, 0)),
                  pl.BlockSpec((None, HEADS_PER_STEP, seq, LANES), lambda b, g, i: (b, g, 0, 0)),
                  pl.BlockSpec((None, n_blocks, width), lambda b, g, i: (b, 0, g))],
        out_specs=pl.BlockSpec((None, tq, width), lambda b, g, i: (b, i, g)),
        out_shape=jax.ShapeDtypeStruct((batch, seq, D_ATT), BF16),
        scratch_shapes=[pltpu.VMEM((HEADS_PER_STEP, tq, 2 * HEAD_DIM), BF16),
                        pltpu.VMEM((HEADS_PER_STEP, tq, 2 * HEAD_DIM), BF16),
                        pltpu.VMEM((HEADS_PER_STEP, tq, 2 * tq), F32),
                        pltpu.VMEM((HEADS_PER_STEP, tq, 2 * tq), BF16),
                        pltpu.VMEM((HEADS_PER_STEP, tq, LANES), F32),
                        pltpu.VMEM((HEADS_PER_STEP, tq, LANES), F32)],
        compiler_params=pltpu.CompilerParams(dimension_semantics=("arbitrary",) * 3, vmem_limit_bytes=VMEM_LIMIT),
        name="moba_prompt",
    )(slopes, q, ka, va, kmean)


def _moba_sample_kernel(n_past_blocks, past_len, pt_ref, q_ref, kn_ref, vnt_ref, slope_ref, tpos_ref, hmask_ref,
                        *rest):
    k_pages = rest[:PAGES_PER_STEP]
    v_pages = rest[PAGES_PER_STEP:2 * PAGES_PER_STEP]
    o_ref, o_all, m_all, l_all, g_all, s_ref, p_ref = rest[2 * PAGES_PER_STEP:]
    del pt_ref
    j = pl.program_id(1)
    n_tok = q_ref.shape[0]
    n_rows = n_tok * N_HEADS
    hmask = hmask_ref[...]
    slope = slope_ref[...]
    tpos = tpos_ref[...]
    q = q_ref[...]
    qbd = jnp.concatenate([jnp.broadcast_to(q[t:t + 1, :], (N_HEADS, D_ATT)) for t in range(n_tok)], axis=0) * hmask
    q_hilo = jnp.concatenate(_hilo(qbd), axis=0)
    colk = lax.broadcasted_iota(jnp.int32, (n_rows, MOBA_BLOCK), 1).astype(F32)
    colb = lax.broadcasted_iota(jnp.int32, (n_rows, LANES), 1)

    @pl.when(j == 0)
    def _():
        m_all[...] = jnp.full(m_all.shape, -jnp.inf, F32)
        l_all[...] = jnp.zeros(l_all.shape, F32)
        g_all[...] = jnp.full(g_all.shape, -jnp.inf, F32)
        p_ref[...] = jnp.zeros(p_ref.shape, BF16)

    def block_t(page_refs, i):
        pages = range(i * PAGES_PER_BLOCK, (i + 1) * PAGES_PER_BLOCK)
        return jnp.concatenate([page_refs[p][...] for p in pages], axis=1).astype(BF16)

    def scores(i):
        kt = block_t(k_pages, i)
        both = jnp.dot(q_hilo, kt, preferred_element_type=F32)
        s_ref[i] = both[:n_rows] + both[n_rows:]

    def softmax(i):
        n = j * BLOCKS_PER_STEP + i
        raw = s_ref[i]
        gate = jnp.mean(raw, axis=-1, keepdims=True)
        dist = (past_len + tpos - (n * MOBA_BLOCK).astype(F32)) - colk
        s = raw - slope * dist
        m = jnp.max(s, axis=-1, keepdims=True)
        p = jnp.exp(s - m)
        l = jnp.sum(p, axis=-1, keepdims=True)
        p_ref[i, :n_rows, :] = p.astype(BF16)
        here = colb == n
        m_all[...] = jnp.where(here, m, m_all[...])
        l_all[...] = jnp.where(here, l, l_all[...])
        g_all[...] = jnp.where(here, gate, g_all[...])

    def values(i):
        o_all[j * BLOCKS_PER_STEP + i] = _dot_nt(block_t(v_pages, i), p_ref[i])

    scores(0)
    for i in range(BLOCKS_PER_STEP):
        if i + 1 < BLOCKS_PER_STEP:
            scores(i + 1)
        softmax(i)
        values(i)

    @pl.when(j == pl.num_programs(1) - 1)
    def _():
        sel = _top_blocks(g_all[...], colb, min(MOBA_TOPK, n_past_blocks))
        kn = kn_ref[...]
        s_own = []
        for t in range(n_tok):
            s_t = jnp.sum(qbd * kn[t:t + 1, :], axis=-1, keepdims=True) - slope * (tpos - float(t))
            s_own.append(jnp.where(tpos >= float(t), s_t, -jnp.inf))
        m_own = functools.reduce(jnp.maximum, s_own)
        m_sel = jnp.max(jnp.where(sel, m_all[...], -jnp.inf), axis=-1, keepdims=True)
        m_tot = jnp.maximum(m_own, m_sel)
        w = jnp.where(sel, jnp.exp(m_all[...] - m_tot), 0.0)
        l_tot = jnp.sum(w * l_all[...], axis=-1, keepdims=True)
        for t in range(n_tok):
            p_t = jnp.exp(s_own[t] - m_tot)
            l_tot = l_tot + p_t
            w = jnp.where(colb == n_past_blocks + t, p_t, w)
        w = jnp.concatenate([w / l_tot, jnp.zeros((SAMPLE_ROWS - n_rows, LANES), F32)], axis=0)
        wt = jnp.transpose(w)
        acc = jnp.zeros((D_ATT, SAMPLE_ROWS), F32)
        for n in range(n_past_blocks):
            acc = acc + o_all[n] * wt[n:n + 1, :]
        vnt = vnt_ref[...]
        for t in range(n_tok):
            acc = acc + vnt[:, t:t + 1] * wt[n_past_blocks + t:n_past_blocks + t + 1, :]
        out = jnp.transpose(acc)[:n_rows] * hmask
        o_ref[...] = jnp.sum(out.reshape(n_tok, N_HEADS, D_ATT), axis=1)


def _moba_sample(layer, page_table, q, k_new, v_new_t, cache_kt, cache_vt, slope_rows, tpos_rows, hmask, past_len):
    dec_batch, n_tok, _ = q.shape
    n_pages = page_table.shape[1]
    assert past_len % MOBA_BLOCK == 0 and n_pages * PAGE_SIZE == past_len and n_pages % PAGES_PER_STEP == 0
    n_past_blocks = past_len // MOBA_BLOCK
    n_rows = n_tok * N_HEADS
    assert SAMPLE_ROWS == LANES and n_rows <= SAMPLE_ROWS and 0 < n_past_blocks <= LANES - n_tok
    assert n_rows % 16 == 0
    n_steps = n_pages // PAGES_PER_STEP

    def tokspec():
        return pl.BlockSpec((None, n_tok, D_ATT), lambda b, j, pt: (b, 0, 0))

    def const(shape):
        return pl.BlockSpec(shape, lambda b, j, pt: (0,) * len(shape))

    def pagespec(i):
        return pl.BlockSpec((None, None, D_ATT, PAGE_SIZE),
                            lambda b, j, pt: (layer, pt[b * n_pages + j * PAGES_PER_STEP + i], 0, 0))

    grid_spec = pltpu.PrefetchScalarGridSpec(
        num_scalar_prefetch=1,
        grid=(dec_batch, n_steps),
        in_specs=[tokspec(), tokspec(), pl.BlockSpec((None, D_ATT, n_tok), lambda b, j, pt: (b, 0, 0)),
                  const((n_rows, 1)), const((n_rows, 1)), const((n_rows, D_ATT))]
        + [pagespec(i) for i in range(PAGES_PER_STEP)] * 2,
        out_specs=tokspec(),
        scratch_shapes=[pltpu.VMEM((n_past_blocks, D_ATT, SAMPLE_ROWS), F32),
                        pltpu.VMEM((n_rows, LANES), F32),
                        pltpu.VMEM((n_rows, LANES), F32),
                        pltpu.VMEM((n_rows, LANES), F32),
                        pltpu.VMEM((BLOCKS_PER_STEP, n_rows, MOBA_BLOCK), F32),
                        pltpu.VMEM((BLOCKS_PER_STEP, SAMPLE_ROWS, MOBA_BLOCK), BF16)],
    )
    return pl.pallas_call(
        functools.partial(_moba_sample_kernel, n_past_blocks, float(past_len)),
        grid_spec=grid_spec,
        out_shape=jax.ShapeDtypeStruct((dec_batch, n_tok, D_ATT), F32),
        compiler_params=pltpu.CompilerParams(dimension_semantics=("arbitrary",) * 2, vmem_limit_bytes=VMEM_LIMIT),
        name="moba_sample",
    )(page_table.reshape(-1), q, k_new, v_new_t, slope_rows, tpos_rows, hmask,
      *([cache_kt] * PAGES_PER_STEP), *([cache_vt] * PAGES_PER_STEP))


def _rglru_coeffs(xc, wa_ref, ba_ref, wx_ref, bx_ref, lam_ref):
    xb = xc.astype(BF16)

    def block_diag(w_ref):
        return jnp.concatenate(
            [jnp.dot(xb[:, n * RNN_BLOCK:(n + 1) * RNN_BLOCK], w_ref[n], preferred_element_type=F32)
             for n in range(N_RNN_BLOCKS)], axis=-1)

    r = jax.nn.sigmoid(block_diag(wa_ref) + ba_ref[...])
    gate_i = jax.nn.sigmoid(block_diag(wx_ref) + bx_ref[...])
    z = -lam_ref[...]
    softplus = jnp.maximum(z, 0.0) + jnp.log1p(jnp.exp(-jnp.abs(z)))
    log_a = -LRU_C * r * softplus
    a = jnp.exp(log_a)
    b = jnp.sqrt(1.0 - a * a) * (gate_i * xc)
    return a, b


def _rglru_prompt_kernel(x_ref, cw_ref, cb_ref, wa_ref, ba_ref, wx_ref, bx_ref, lam_ref,
                         y_ref, conv_ref, hlast_ref, xtail, hcarry):
    t = pl.program_id(1)
    tt = x_ref.shape[0]

    @pl.when(t == 0)
    def _():
        xtail[...] = jnp.zeros(xtail.shape, F32)
        hcarry[...] = jnp.zeros(hcarry.shape, F32)

    n_grp = tt // SCAN_GROUP
    x = x_ref[...].reshape(n_grp, SCAN_GROUP, D_RNN)
    tail = xtail[...]
    row = lax.broadcasted_iota(jnp.int32, (n_grp, SCAN_GROUP, D_RNN), 1)
    xc = cb_ref[...] + x * cw_ref[CONV_W - 1:CONV_W, :]
    for d in range(1, CONV_W):
        rot = pltpu.roll(x, d, 1)
        before = jnp.concatenate([pltpu.roll(tail, d, 0)[None], rot[:-1]], axis=0)
        xs = jnp.where(row < d, before, rot)
        xc = xc + xs * cw_ref[CONV_W - 1 - d:CONV_W - d, :]
    a, b = _rglru_coeffs(xc.reshape(tt, D_RNN), wa_ref, ba_ref, wx_ref, bx_ref, lam_ref)
    a = a.reshape(n_grp, SCAN_GROUP, D_RNN)
    b = b.reshape(n_grp, SCAN_GROUP, D_RNN)
    d = 1
    while d < SCAN_GROUP:
        keep = row >= d
        b = jnp.where(keep, a * pltpu.roll(b, d, 1) + b, b)
        a = jnp.where(keep, a * pltpu.roll(a, d, 1), a)
        d *= 2
    carry = hcarry[...]
    for g2 in range(n_grp // 2):
        h_pair = []
        for g in (2 * g2, 2 * g2 + 1):
            h_pair.append(a[g] * carry + b[g])
            carry = h_pair[-1][SCAN_GROUP - 1:SCAN_GROUP]
        y_ref[2 * g2 * SCAN_GROUP:2 * (g2 + 1) * SCAN_GROUP, :] = jnp.concatenate(h_pair, axis=0).astype(y_ref.dtype)
    xtail[...] = x[n_grp - 1]
    hcarry[...] = carry

    @pl.when(t == pl.num_programs(1) - 1)
    def _():
        conv_ref[...] = x_ref[tt - (CONV_W - 1):tt, :]
        hlast_ref[...] = carry


def _rglru_prompt(xr, conv_w, conv_b, wa_b, b_a, wx_b, b_x, lam):
    batch, seq, _ = xr.shape
    tt = 256
    assert seq % tt == 0 and tt % (2 * SCAN_GROUP) == 0

    def const(shape):
        return pl.BlockSpec(shape, lambda b, t: (0,) * len(shape))

    wspec = const((N_RNN_BLOCKS, RNN_BLOCK, RNN_BLOCK))
    vec = const((1, D_RNN))
    return pl.pallas_call(
        _rglru_prompt_kernel,
        grid=(batch, seq // tt),
        in_specs=[pl.BlockSpec((None, tt, D_RNN), lambda b, t: (b, t, 0)),
                  const((CONV_W, D_RNN)), vec, wspec, vec, wspec, vec, vec],
        out_specs=[pl.BlockSpec((None, tt, D_RNN), lambda b, t: (b, t, 0)),
                   pl.BlockSpec((None, CONV_W - 1, D_RNN), lambda b, t: (b, 0, 0)),
                   pl.BlockSpec((None, 1, D_RNN), lambda b, t: (b, 0, 0))],
        out_shape=[jax.ShapeDtypeStruct((batch, seq, D_RNN), BF16),
                   jax.ShapeDtypeStruct((batch, CONV_W - 1, D_RNN), F32),
                   jax.ShapeDtypeStruct((batch, 1, D_RNN), F32)],
        scratch_shapes=[pltpu.VMEM((8, D_RNN), F32), pltpu.VMEM((1, D_RNN), F32)],
        compiler_params=pltpu.CompilerParams(dimension_semantics=("arbitrary",) * 2, vmem_limit_bytes=VMEM_LIMIT),
        name="rglru_prompt",
    )(xr, conv_w, conv_b, wa_b, b_a, wx_b, b_x, lam)


def _rglru_sample_kernel(x_ref, prev_ref, h0_ref, cw_ref, cb_ref, wa_ref, ba_ref, wx_ref, bx_ref, lam_ref,
                         y_ref, hlast_ref):
    n_tok, n_seq = x_ref.shape[0], x_ref.shape[1]
    xp = [prev_ref[i] for i in range(CONV_W - 1)] + [x_ref[i] for i in range(n_tok)]
    xc = [cb_ref[...] + functools.reduce(lambda u, w: u + w, [xp[t + j] * cw_ref[j:j + 1, :] for j in range(CONV_W)])
          for t in range(n_tok)]
    a, b = _rglru_coeffs(jnp.concatenate(xc, axis=0), wa_ref, ba_ref, wx_ref, bx_ref, lam_ref)
    h = h0_ref[...]
    for t in range(n_tok):
        h = a[t * n_seq:(t + 1) * n_seq] * h + b[t * n_seq:(t + 1) * n_seq]
        y_ref[t] = h
    hlast_ref[...] = h


def _rglru_sample(xr_t, prev_t, h0, conv_w, conv_b, wa_b, b_a, wx_b, b_x, lam):
    n_tok, n_seq, _ = xr_t.shape
    assert n_seq % 8 == 0
    return pl.pallas_call(
        _rglru_sample_kernel,
        out_shape=[jax.ShapeDtypeStruct((n_tok, n_seq, D_RNN), F32), jax.ShapeDtypeStruct((n_seq, D_RNN), F32)],
        compiler_params=pltpu.CompilerParams(vmem_limit_bytes=VMEM_LIMIT),
        name="rglru_sample",
    )(xr_t, prev_t, h0, conv_w, conv_b, wa_b, b_a, wx_b, b_x, lam)


def _merge_kernel(x_ref, att_ref, ga_ref, rnn_ref, gr_ref, ma_ref, mr_ref, wao_ref, wro_ref, wo_ref, y_ref):
    def f32(ref):
        return ref[...].astype(F32)

    y_a = jnp.dot((f32(att_ref) * jax.nn.silu(f32(ga_ref))).astype(BF16), wao_ref[...], preferred_element_type=F32)
    y_r = jnp.dot((f32(rnn_ref) * jax.nn.silu(f32(gr_ref))).astype(BF16), wro_ref[...], preferred_element_type=F32)
    z = jax.nn.sigmoid(f32(ma_ref)) * y_a + jax.nn.sigmoid(f32(mr_ref)) * y_r
    y_ref[...] = x_ref[...] + jnp.dot(z.astype(BF16), wo_ref[...], preferred_element_type=F32)


def _merge(x, att, ga, rnn, gr, ma, mr, wao_b, wro_b, wo_b):
    rows = x.shape[0]
    tm = min(rows, 256)
    assert rows % tm == 0

    def rowspec(width):
        return pl.BlockSpec((tm, width), lambda i: (i, 0))

    def const(shape):
        return pl.BlockSpec(shape, lambda i: (0,) * len(shape))

    return pl.pallas_call(
        _merge_kernel,
        grid=(rows // tm,),
        in_specs=[rowspec(D_MODEL), rowspec(D_ATT), rowspec(D_ATT), rowspec(D_RNN), rowspec(D_RNN),
                  rowspec(D_MODEL), rowspec(D_MODEL),
                  const((D_ATT, D_MODEL)), const((D_RNN, D_MODEL)), const((D_MODEL, D_MODEL))],
        out_specs=rowspec(D_MODEL),
        out_shape=jax.ShapeDtypeStruct((rows, D_MODEL), F32),
        compiler_params=pltpu.CompilerParams(dimension_semantics=("arbitrary",), vmem_limit_bytes=VMEM_LIMIT),
        name="merge",
    )(x, att, ga, rnn, gr, ma, mr, wao_b, wro_b, wo_b)


def kernel(x_prompt, x_sample, cache_k, cache_v, state_conv, state_h, page_table, norm_w, w_in, q_norm_w, k_norm_w,
           conv_w, conv_b, w_gate_a, b_gate_a, w_gate_x, b_gate_x, lru_lambda, w_attn_out, w_rnn_out, w_out):
    batch, seq, _ = x_prompt.shape
    dec_batch, dec_seq, _ = x_sample.shape
    depth = w_in.shape[0]
    n_pool = cache_k.shape[1]
    past_len = page_table.shape[1] * PAGE_SIZE
    n_blocks = seq // MOBA_BLOCK

    slopes = jnp.asarray([2.0 ** (-8.0 * (h + 1) / N_HEADS) for h in range(N_HEADS)], F32)
    head_of_lane = np.arange(D_ATT) // HEAD_DIM
    gmean = jnp.asarray((head_of_lane[:, None] == head_of_lane[None, :]) / HEAD_DIM, BF16)
    row_head = np.arange(dec_seq * N_HEADS) % N_HEADS
    slope_rows = jnp.asarray(2.0 ** (-8.0 * (row_head + 1) / N_HEADS), F32)[:, None]
    tpos_rows = jnp.asarray(np.arange(dec_seq * N_HEADS) // N_HEADS, F32)[:, None]
    hmask = jnp.asarray(row_head[:, None] == head_of_lane[None, :], F32)

    w_in_b = w_in.astype(BF16)
    wa_b = w_gate_a.astype(BF16)
    wx_b = w_gate_x.astype(BF16)
    wao_b = w_attn_out.astype(BF16)
    wro_b = w_rnn_out.astype(BF16)
    wo_b = w_out.astype(BF16)
    cache_kt = jnp.transpose(cache_k, (0, 1, 3, 4, 2)).reshape(depth, n_pool, D_ATT, PAGE_SIZE)
    cache_vt = jnp.transpose(cache_v, (0, 1, 3, 4, 2)).reshape(depth, n_pool, D_ATT, PAGE_SIZE)

    y_p = x_prompt.reshape(batch * seq, D_MODEL)
    y_s = x_sample.reshape(dec_batch * dec_seq, D_MODEL)
    outs = {name: [] for name in ("cp", "hp", "ks", "vs", "cs", "hs")}
    kt_all = vt_all = None
    for l in range(depth):
        nw = norm_w[l][None, :]
        qn_t = jnp.tile(q_norm_w[l], N_HEADS)[None, :]
        kn_t = jnp.tile(k_norm_w[l], N_HEADS)[None, :]
        rnn_w = (conv_w[l], conv_b[l][None, :], wa_b[l], b_gate_a[l][None, :], wx_b[l], b_gate_x[l][None, :],
                 lru_lambda[l][None, :])
        out_w = (wao_b[l], wro_b[l], wo_b[l])

        q, kt_all, vt_all, ka, va, kmean, ga, xr, gr, ma, mr = _proj_prompt(
            y_p, batch, l, depth, None if l == 0 else (kt_all, vt_all), nw, w_in_b[l], qn_t, kn_t, gmean)
        att = _moba_prompt(slopes, q.reshape(batch, seq, D_ATT),
                           ka.reshape(batch, n_blocks, N_HEADS, 2 * HEAD_DIM, MOBA_BLOCK), va,
                           kmean.reshape(batch, n_blocks, D_ATT))
        rnn, c_new, h_new = _rglru_prompt(xr.reshape(batch, seq, D_RNN), *rnn_w)
        y_p = _merge(y_p, att.reshape(batch * seq, D_ATT), ga, rnn.reshape(batch * seq, D_RNN), gr, ma, mr, *out_w)
        outs["cp"].append(c_new)
        outs["hp"].append(h_new.reshape(batch, D_RNN))

        q, k, v, ga, xr, gr, ma, mr = _proj_sample(y_s, nw, w_in_b[l], qn_t, kn_t, gmean)
        att = _moba_sample(l, page_table, q.reshape(dec_batch, dec_seq, D_ATT), k.reshape(dec_batch, dec_seq, D_ATT),
                           jnp.swapaxes(v.reshape(dec_batch, dec_seq, D_ATT), 1, 2), cache_kt, cache_vt,
                           slope_rows, tpos_rows, hmask, past_len)
        xr_t = jnp.swapaxes(xr.reshape(dec_batch, dec_seq, D_RNN), 0, 1)
        prev_t = jnp.swapaxes(state_conv[l], 0, 1)
        rnn_t, h_new = _rglru_sample(xr_t, prev_t, state_h[l], *rnn_w)
        rnn = jnp.swapaxes(rnn_t, 0, 1).reshape(dec_batch * dec_seq, D_RNN)
        c_new = jnp.swapaxes(jnp.concatenate([prev_t, xr_t], axis=0)[dec_seq:], 0, 1)
        y_s = _merge(y_s, att.reshape(dec_batch * dec_seq, D_ATT), ga, rnn, gr, ma, mr, *out_w)
        outs["ks"].append(k.reshape(dec_batch, dec_seq, N_HEADS, HEAD_DIM))
        outs["vs"].append(v.reshape(dec_batch, dec_seq, N_HEADS, HEAD_DIM))
        outs["cs"].append(c_new)
        outs["hs"].append(h_new)

    def from_transposed(t):
        return jnp.transpose(t.reshape(depth, batch, N_HEADS, HEAD_DIM, seq), (0, 1, 4, 2, 3))

    return (y_p.reshape(batch, seq, D_MODEL), y_s.reshape(dec_batch, dec_seq, D_MODEL),
            from_transposed(kt_all), from_transposed(vt_all), jnp.stack(outs["cp"]), jnp.stack(outs["hp"]),
            jnp.stack(outs["ks"]), jnp.stack(outs["vs"]), jnp.stack(outs["cs"]), jnp.stack(outs["hs"]))
```

```python
import functools
import math

import numpy as np
import jax
import jax.numpy as jnp
from jax import lax
from jax.experimental import pallas as pl
from jax.experimental.pallas import tpu as pltpu

F32 = jnp.float32
BF16 = jnp.bfloat16

D_MODEL = 1024
N_HEADS = 8
HEAD_DIM = 64
D_ATT = N_HEADS * HEAD_DIM
D_RNN = D_MODEL
N_RNN_BLOCKS = 8
RNN_BLOCK = D_RNN // N_RNN_BLOCKS
CONV_W = 4
LRU_C = 8.0
MOBA_BLOCK = 256
MOBA_TOPK = 3
PAGE_SIZE = 128
RMS_EPS = 1e-6
SPLITS = (D_ATT, D_ATT, D_ATT, D_ATT, D_RNN, D_RNN, D_MODEL, D_MODEL)
N_IN = sum(SPLITS)
CUTS = tuple(int(c) for c in np.cumsum((0,) + SPLITS))

LANES = 128
HEADS_PER_SLAB = LANES // HEAD_DIM
N_SLABS = D_ATT // LANES
MASK_BIAS = -1e30
VMEM_LIMIT = 56 * 1024 * 1024

PAGES_PER_STEP = 16
BLOCKS_PER_STEP = PAGES_PER_STEP * PAGE_SIZE // MOBA_BLOCK
PAGES_PER_BLOCK = MOBA_BLOCK // PAGE_SIZE
SAMPLE_ROWS = LANES

AUG_ROW_ONES_I = 0
AUG_ROW_ONES_QB = 1
AUG_ROW_KEY = 2
AUG_ROW_BLOCK = 3
AUG_ROWS = 4
ROW_CHUNK = 64
HEADS_PER_STEP = 8
PAIRS_PER_TRIP = 2
SCAN_GROUP = 8
assert all(math.frexp(2.0 ** (-8.0 * (h + 1) / N_HEADS))[0] == 0.5 for h in range(N_HEADS)) and MOBA_BLOCK <= 256


def _hilo(x):
    hi = x.astype(BF16)
    lo = (x - hi.astype(F32)).astype(BF16)
    return hi, lo


def _dot_nt(a, b):
    return lax.dot_general(a, b, (((1,), (1,)), ((), ())), preferred_element_type=F32)


def _top_blocks(gate, col, n_sel):
    sel = jnp.zeros(gate.shape, jnp.bool_)
    colf = col.astype(F32)
    for _ in range(n_sel):
        m = jnp.max(gate, axis=-1, keepdims=True)
        idx = jnp.min(jnp.where(gate == m, colf, float(gate.shape[-1])), axis=-1, keepdims=True)
        hit = colf == idx
        sel = jnp.logical_or(sel, jnp.logical_and(hit, m > -jnp.inf))
        gate = jnp.where(hit, -jnp.inf, gate)
    return sel


def _proj_parts(x_ref, nw_ref, w_ref, qn_ref, kn_ref, gmean_ref):
    x = x_ref[...]
    ms = jnp.mean(x * x, axis=-1, keepdims=True)
    h = (x * lax.rsqrt(ms + RMS_EPS) * nw_ref[...]).astype(BF16)

    def part(i):
        return jnp.dot(h, w_ref[:, CUTS[i]:CUTS[i + 1]], preferred_element_type=F32)

    zq, zk = part(0), part(1)
    rows = zq.shape[0]
    sq = jnp.concatenate(_hilo(zq * zq) + _hilo(zk * zk), axis=0)
    ms_h = jnp.dot(sq, gmean_ref[...], preferred_element_type=F32)
    ms_q = ms_h[:rows] + ms_h[rows:2 * rows]
    ms_k = ms_h[2 * rows:3 * rows] + ms_h[3 * rows:]
    q = zq * lax.rsqrt(ms_q + RMS_EPS) * qn_ref[...] * (HEAD_DIM ** -0.5)
    k = zk * lax.rsqrt(ms_k + RMS_EPS) * kn_ref[...]
    return q, k, part


def _proj_sample_kernel(x_ref, nw_ref, w_ref, qn_ref, kn_ref, gmean_ref,
                        q_ref, k_ref, v_ref, ga_ref, xr_ref, gr_ref, ma_ref, mr_ref):
    q, k, part = _proj_parts(x_ref, nw_ref, w_ref, qn_ref, kn_ref, gmean_ref)
    q_ref[...] = q
    k_ref[...] = k
    for i, ref in zip(range(2, 8), (v_ref, ga_ref, xr_ref, gr_ref, ma_ref, mr_ref)):
        ref[...] = part(i)


def _proj_prompt_kernel(n_blocks, n_aliased, x_ref, nw_ref, w_ref, qn_ref, kn_ref, gmean_ref, *rest):
    q_ref, kt_ref, vt_ref, ka_ref, va_ref, km_ref, ga_ref, xr_ref, gr_ref, ma_ref, mr_ref = rest[n_aliased:]
    q, k, part = _proj_parts(x_ref, nw_ref, w_ref, qn_ref, kn_ref, gmean_ref)
    v = part(2)
    q_ref[...] = q
    for i, ref in zip(range(3, 8), (ga_ref, xr_ref, gr_ref, ma_ref, mr_ref)):
        ref[...] = part(i).astype(ref.dtype)
    km_ref[...] = jnp.mean(k, axis=0, keepdims=True)
    kt = jnp.transpose(k)
    kt_ref[...] = kt
    vt_ref[...] = jnp.transpose(v)
    n = pl.program_id(0) % n_blocks
    r = lax.broadcasted_iota(jnp.int32, (HEAD_DIM, MOBA_BLOCK), 0)
    j = lax.broadcasted_iota(jnp.int32, (HEAD_DIM, MOBA_BLOCK), 1).astype(F32)
    ones_rows = (r == n) | (r == n_blocks + AUG_ROW_ONES_I) | (r == n_blocks + AUG_ROW_ONES_QB)
    lane = lax.broadcasted_iota(jnp.int32, (MOBA_BLOCK, LANES), 1)
    for h in range(N_HEADS):
        slope = 2.0 ** (-8.0 * (h + 1) / N_HEADS)
        extra = jnp.where(ones_rows, 1.0, 0.0)
        extra = jnp.where(r == n_blocks + AUG_ROW_KEY, slope * j, extra)
        extra = jnp.where(r == n_blocks + AUG_ROW_BLOCK, slope * (n * MOBA_BLOCK).astype(F32), extra)
        ka_ref[h] = jnp.concatenate([kt[h * HEAD_DIM:(h + 1) * HEAD_DIM].astype(BF16), extra.astype(BF16)], axis=0)
        slab = v[:, (h // HEADS_PER_SLAB) * LANES:(h // HEADS_PER_SLAB + 1) * LANES]
        own = (lane // HEAD_DIM) == (h % HEADS_PER_SLAB)
        va_ref[h] = jnp.where(own, slab, 1.0).astype(BF16)


def _proj_in_specs(tm):
    def const(shape):
        return pl.BlockSpec(shape, lambda i: (0,) * len(shape))

    return [pl.BlockSpec((tm, D_MODEL), lambda i: (i, 0)), const((1, D_MODEL)), const((D_MODEL, N_IN)),
            const((1, D_ATT)), const((1, D_ATT)), const((D_ATT, D_ATT))]


def _proj_sample(x, norm_w, w_in_b, qn_t, kn_t, gmean):
    rows = x.shape[0]
    widths = (D_ATT, D_ATT, D_ATT, D_ATT, D_RNN, D_RNN, D_MODEL, D_MODEL)
    return pl.pallas_call(
        _proj_sample_kernel,
        grid=(1,),
        in_specs=_proj_in_specs(rows),
        out_specs=[pl.BlockSpec((rows, w), lambda i: (i, 0)) for w in widths],
        out_shape=[jax.ShapeDtypeStruct((rows, w), F32) for w in widths],
        compiler_params=pltpu.CompilerParams(dimension_semantics=("arbitrary",), vmem_limit_bytes=VMEM_LIMIT),
        name="proj_sample",
    )(x, norm_w, w_in_b, qn_t, kn_t, gmean)


def _proj_prompt(x, batch, layer, depth, kv_all, norm_w, w_in_b, qn_t, kn_t, gmean):
    rows = x.shape[0]
    seq = rows // batch
    tm = MOBA_BLOCK
    n_blocks = seq // tm
    assert seq % tm == 0 and n_blocks + AUG_ROWS <= HEAD_DIM
    kv_all = () if kv_all is None else tuple(kv_all)
    n_fixed = len(_proj_in_specs(tm))

    def rowspec(width):
        return pl.BlockSpec((tm, width), lambda i: (i, 0))

    def tspec():
        return pl.BlockSpec((None, None, D_ATT, tm), lambda i: (layer, i // n_blocks, 0, i % n_blocks))

    widths = (D_ATT, D_RNN, D_RNN, D_MODEL, D_MODEL)
    out_specs = [rowspec(D_ATT), tspec(), tspec(),
                 pl.BlockSpec((None, N_HEADS, 2 * HEAD_DIM, tm), lambda i: (i, 0, 0, 0)),
                 pl.BlockSpec((None, N_HEADS, tm, LANES), lambda i: (i // n_blocks, 0, i % n_blocks, 0)),
                 pl.BlockSpec((None, 1, D_ATT), lambda i: (i, 0, 0))] + [rowspec(w) for w in widths]
    out_shape = [jax.ShapeDtypeStruct((rows, D_ATT), F32),
                 jax.ShapeDtypeStruct((depth, batch, D_ATT, seq), F32),
                 jax.ShapeDtypeStruct((depth, batch, D_ATT, seq), F32),
                 jax.ShapeDtypeStruct((batch * n_blocks, N_HEADS, 2 * HEAD_DIM, tm), BF16),
                 jax.ShapeDtypeStruct((batch, N_HEADS, seq, LANES), BF16),
                 jax.ShapeDtypeStruct((batch * n_blocks, 1, D_ATT), F32)] + [
                     jax.ShapeDtypeStruct((rows, w), d) for w, d in zip(widths, (BF16, F32, BF16, BF16, BF16))]
    return pl.pallas_call(
        functools.partial(_proj_prompt_kernel, n_blocks, len(kv_all)),
        grid=(rows // tm,),
        in_specs=_proj_in_specs(tm) + [pl.BlockSpec(memory_space=pl.ANY)] * len(kv_all),
        out_specs=out_specs,
        out_shape=out_shape,
        input_output_aliases={n_fixed + i: 1 + i for i in range(len(kv_all))},
        compiler_params=pltpu.CompilerParams(dimension_semantics=("arbitrary",), vmem_limit_bytes=VMEM_LIMIT),
        name="proj_prompt",
    )(x, norm_w, w_in_b, qn_t, kn_t, gmean, *kv_all)


def _top_blocks_t(gate, n_sel):
    sel = jnp.zeros(gate.shape, jnp.bool_)
    blk = lax.broadcasted_iota(jnp.int32, gate.shape, gate.ndim - 2).astype(F32)
    for _ in range(n_sel):
        m = jnp.max(gate, axis=-2, keepdims=True)
        idx = jnp.min(jnp.where(gate == m, blk, float(gate.shape[-2])), axis=-2, keepdims=True)
        hit = blk == idx
        sel = jnp.logical_or(sel, jnp.logical_and(hit, m > -jnp.inf))
        gate = jnp.where(hit, -jnp.inf, gate)
    return sel


def _moba_prompt_kernel(slopes_ref, q_ref, ka_ref, va_ref, km_ref, o_ref,
                        qa_past_ref, qa_own_ref, s_ref, p_ref, m_ref, acc_ref):
    qb = pl.program_id(2)
    n_blocks = km_ref.shape[0]
    tq = MOBA_BLOCK
    lane = lax.broadcasted_iota(jnp.int32, (tq, LANES), 1)
    lane_km = lax.broadcasted_iota(jnp.int32, (n_blocks, HEADS_PER_STEP * HEAD_DIM), 1)
    xrow = lax.broadcasted_iota(jnp.int32, (HEAD_DIM - n_blocks, tq), 0)
    qpos = lax.broadcasted_iota(jnp.int32, (HEAD_DIM - n_blocks, tq), 1).astype(F32)
    q = q_ref[...]
    q_hi, q_lo = _hilo(q)
    km = km_ref[...]
    qb_f = (qb * tq).astype(F32)

    def head_dims(hh):
        q_pair = q[:, (hh // 2) * LANES:(hh // 2 + 1) * LANES]
        return q_pair if hh % 2 == 0 else pltpu.roll(q_pair, HEAD_DIM, 1)

    aug = lane - HEAD_DIM - n_blocks
    rowi = lax.broadcasted_iota(jnp.int32, (tq, LANES), 0).astype(F32)
    for hh in range(HEADS_PER_STEP):
        slope = slopes_ref[pl.program_id(1) * HEADS_PER_STEP + hh]
        cols = jnp.where(aug == AUG_ROW_ONES_I, -slope * rowi, 0.0)
        cols = jnp.where(aug == AUG_ROW_ONES_QB, -slope * qb_f, cols)
        cols = jnp.where((aug == AUG_ROW_KEY) | (aug == AUG_ROW_BLOCK), 1.0, cols)
        qa_own_ref[hh] = jnp.where(lane < HEAD_DIM, head_dims(hh), cols).astype(BF16)
        m_ref[hh] = jnp.full((tq, LANES), -jnp.inf, F32)
        acc_ref[hh] = jnp.zeros((tq, LANES), F32)

    def select_past_blocks():
        km_heads = jnp.concatenate([jnp.where((lane_km // HEAD_DIM) == hh, km, 0.0) for hh in range(HEADS_PER_STEP)],
                                   axis=0)
        km_hi, km_lo = _hilo(km_heads)
        gate = _dot_nt(km_hi, q_hi) + _dot_nt(km_hi, q_lo) + _dot_nt(km_lo, q_hi)
        gate = gate.reshape(HEADS_PER_STEP, n_blocks, tq)
        blk = lax.broadcasted_iota(jnp.int32, gate.shape, 1)
        sel = _top_blocks_t(jnp.where(blk < qb, gate, -jnp.inf), min(MOBA_TOPK, n_blocks - 1))
        bias_t = jnp.where(sel, 0.0, MASK_BIAS)
        for hh in range(HEADS_PER_STEP):
            slope = slopes_ref[pl.program_id(1) * HEADS_PER_STEP + hh]
            extra_t = jnp.where(xrow == AUG_ROW_ONES_I, -slope * qpos, 0.0)
            extra_t = jnp.where(xrow == AUG_ROW_ONES_QB, -slope * qb_f, extra_t)
            extra_t = jnp.where((xrow == AUG_ROW_KEY) | (xrow == AUG_ROW_BLOCK), 1.0, extra_t)
            cols = jnp.transpose(jnp.concatenate([jnp.zeros((HEAD_DIM, tq), F32), bias_t[hh], extra_t], axis=0))
            qa_past_ref[hh] = jnp.where(lane < HEAD_DIM, head_dims(hh), cols).astype(BF16)

    def update(qa_ref, starts, n_blk, causal):
        tiles = n_blk * tq // LANES

        def scores(n0, hh):
            for b in range(n_blk):
                s_ref[hh, :, b * tq:(b + 1) * tq] = jnp.dot(qa_ref[hh], ka_ref[n0 + b, hh],
                                                            preferred_element_type=F32)

        def softmax(n0, hh):
            for c in range(tq // ROW_CHUNK):
                rows = slice(c * ROW_CHUNK, (c + 1) * ROW_CHUNK)
                m_old = m_ref[hh, rows, :]
                s = [s_ref[hh, rows, t * LANES:(t + 1) * LANES] for t in range(tiles)]
                if causal:
                    key = lax.broadcasted_iota(jnp.int32, (ROW_CHUNK, LANES), 1)
                    qry = lax.broadcasted_iota(jnp.int32, (ROW_CHUNK, LANES), 0) + c * ROW_CHUNK
                    s = [jnp.where(key + t * LANES <= qry, s[t], -jnp.inf) for t in range(tiles)]
                m_new = jnp.maximum(m_old, jnp.max(functools.reduce(jnp.maximum, s), axis=-1, keepdims=True))
                for t in range(tiles):
                    p_ref[hh, rows, t * LANES:(t + 1) * LANES] = jnp.exp(s[t] - m_new).astype(BF16)
                acc_ref[hh, rows, :] = acc_ref[hh, rows, :] * jnp.exp(m_old - m_new)
                m_ref[hh, rows, :] = m_new

        def values(n0, hh):
            off = pl.multiple_of(n0 * tq, tq)
            acc_ref[hh] += jnp.dot(p_ref[hh, :, :n_blk * tq], va_ref[hh, pl.ds(off, n_blk * tq), :],
                                   preferred_element_type=F32)

        units = [(n0, hh) for n0 in starts for hh in range(HEADS_PER_STEP)]
        scores(*units[0])
        for u, unit in enumerate(units):
            if u + 1 < len(units):
                scores(*units[u + 1])
            softmax(*unit)
            values(*unit)

    update(qa_own_ref, [qb], 1, True)
    select_past_blocks()

    n_pairs = (qb + 1) // 2

    def body(i, carry):
        update(qa_past_ref, [2 * (PAIRS_PER_TRIP * i + k) for k in range(PAIRS_PER_TRIP)], 2, False)
        return carry

    lax.fori_loop(0, n_pairs // PAIRS_PER_TRIP, body, 0)
    if PAIRS_PER_TRIP > 1:
        def tail(i, carry):
            update(qa_past_ref, [2 * i], 2, False)
            return carry

        lax.fori_loop((n_pairs // PAIRS_PER_TRIP) * PAIRS_PER_TRIP, n_pairs, tail, 0)

    for hh in range(HEADS_PER_STEP):
        acc = acc_ref[hh]
        o_h = acc / pltpu.roll(acc, HEAD_DIM, 1)
        if hh % 2 == 1:
            pair = jnp.where(lane < HEAD_DIM, o_prev, o_h)
            o_ref[:, (hh // 2) * LANES:(hh // 2 + 1) * LANES] = pair.astype(o_ref.dtype)
        o_prev = o_h


def _moba_prompt(slopes, q, ka, va, kmean):
    batch, seq, _ = q.shape
    n_blocks = seq // MOBA_BLOCK
    assert seq % MOBA_BLOCK == 0 and n_blocks > MOBA_TOPK and n_blocks % 2 == 0
    assert HEADS_PER_SLAB == 2 and HEADS_PER_STEP % 2 == 0 and N_HEADS % HEADS_PER_STEP == 0
    tq = MOBA_BLOCK
    width = HEADS_PER_STEP * HEAD_DIM
    return pl.pallas_call(
        _moba_prompt_kernel,
        grid=(batch, N_HEADS // HEADS_PER_STEP, n_blocks),
        in_specs=[pl.BlockSpec(memory_space=pltpu.SMEM),
                  pl.BlockSpec((None, tq, width), lambda b, g, i: (b, i, g)),
                  pl.BlockSpec((None, n_blocks, HEADS_PER_STEP, 2 * HEAD_DIM, tq), lambda b, g, i: (b, 0, g, 0, 0),
                               pipeline_mode=pl.Buffered(1)),
                  pl.BlockSpec((None, HEADS_PER_STEP, seq, LANES), lambda b, g, i: (b, g, 0, 0),
                               pipeline_mode=pl.Buffered(1)),
                  pl.BlockSpec((None, n_blocks, width), lambda b, g, i: (b, 0, g))],
        out_specs=pl.BlockSpec((None, tq, width), lambda b, g, i: (b, i, g)),
        out_shape=jax.ShapeDtypeStruct((batch, seq, D_ATT), BF16),
        scratch_shapes=[pltpu.VMEM((HEADS_PER_STEP, tq, 2 * HEAD_DIM), BF16),
                        pltpu.VMEM((HEADS_PER_STEP, tq, 2 * HEAD_DIM), BF16),
                        pltpu.VMEM((HEADS_PER_STEP, tq, 2 * tq), F32),
                        pltpu.VMEM((HEADS_PER_STEP, tq, 2 * tq), BF16),
                        pltpu.VMEM((HEADS_PER_STEP, tq, LANES), F32),
                        pltpu.VMEM((HEADS_PER_STEP, tq, LANES), F32)],
        compiler_params=pltpu.CompilerParams(dimension_semantics=("arbitrary",) * 3, vmem_limit_bytes=VMEM_LIMIT),
        name="moba_prompt",
    )(slopes, q, ka, va, kmean)


def _moba_sample_kernel(n_past_blocks, past_len, pt_ref, q_ref, kn_ref, vnt_ref, slope_ref, tpos_ref, hmask_ref,
                        *rest):
    k_pages = rest[:PAGES_PER_STEP]
    v_pages = rest[PAGES_PER_STEP:2 * PAGES_PER_STEP]
    o_ref, o_all, m_all, l_all, g_all, s_ref, p_ref = rest[2 * PAGES_PER_STEP:]
    del pt_ref
    j = pl.program_id(1)
    n_tok = q_ref.shape[0]
    n_rows = n_tok * N_HEADS
    hmask = hmask_ref[...]
    slope = slope_ref[...]
    tpos = tpos_ref[...]
    q = q_ref[...]
    qbd = jnp.concatenate([jnp.broadcast_to(q[t:t + 1, :], (N_HEADS, D_ATT)) for t in range(n_tok)], axis=0) * hmask
    q_hilo = jnp.concatenate(_hilo(qbd), axis=0)
    colk = lax.broadcasted_iota(jnp.int32, (n_rows, MOBA_BLOCK), 1).astype(F32)
    colb = lax.broadcasted_iota(jnp.int32, (n_rows, LANES), 1)

    @pl.when(j == 0)
    def _():
        m_all[...] = jnp.full(m_all.shape, -jnp.inf, F32)
        l_all[...] = jnp.zeros(l_all.shape, F32)
        g_all[...] = jnp.full(g_all.shape, -jnp.inf, F32)
        p_ref[...] = jnp.zeros(p_ref.shape, BF16)

    def block_t(page_refs, i):
        pages = range(i * PAGES_PER_BLOCK, (i + 1) * PAGES_PER_BLOCK)
        return jnp.concatenate([page_refs[p][...] for p in pages], axis=1).astype(BF16)

    def scores(i):
        kt = block_t(k_pages, i)
        both = jnp.dot(q_hilo, kt, preferred_element_type=F32)
        s_ref[i] = both[:n_rows] + both[n_rows:]

    def softmax(i):
        n = j * BLOCKS_PER_STEP + i
        raw = s_ref[i]
        gate = jnp.mean(raw, axis=-1, keepdims=True)
        dist = (past_len + tpos - (n * MOBA_BLOCK).astype(F32)) - colk
        s = raw - slope * dist
        m = jnp.max(s, axis=-1, keepdims=True)
        p = jnp.exp(s - m)
        l = jnp.sum(p, axis=-1, keepdims=True)
        p_ref[i, :n_rows, :] = p.astype(BF16)
        here = colb == n
        m_all[...] = jnp.where(here, m, m_all[...])
        l_all[...] = jnp.where(here, l, l_all[...])
        g_all[...] = jnp.where(here, gate, g_all[...])

    def values(i):
        o_all[j * BLOCKS_PER_STEP + i] = _dot_nt(block_t(v_pages, i), p_ref[i])

    scores(0)
    for i in range(BLOCKS_PER_STEP):
        if i + 1 < BLOCKS_PER_STEP:
            scores(i + 1)
        softmax(i)
        values(i)

    @pl.when(j == pl.num_programs(1) - 1)
    def _():
        sel = _top_blocks(g_all[...], colb, min(MOBA_TOPK, n_past_blocks))
        kn = kn_ref[...]
        s_own = []
        for t in range(n_tok):
            s_t = jnp.sum(qbd * kn[t:t + 1, :], axis=-1, keepdims=True) - slope * (tpos - float(t))
            s_own.append(jnp.where(tpos >= float(t), s_t, -jnp.inf))
        m_own = functools.reduce(jnp.maximum, s_own)
        m_sel = jnp.max(jnp.where(sel, m_all[...], -jnp.inf), axis=-1, keepdims=True)
        m_tot = jnp.maximum(m_own, m_sel)
        w = jnp.where(sel, jnp.exp(m_all[...] - m_tot), 0.0)
        l_tot = jnp.sum(w * l_all[...], axis=-1, keepdims=True)
        for t in range(n_tok):
            p_t = jnp.exp(s_own[t] - m_tot)
            l_tot = l_tot + p_t
            w = jnp.where(colb == n_past_blocks + t, p_t, w)
        w = jnp.concatenate([w / l_tot, jnp.zeros((SAMPLE_ROWS - n_rows, LANES), F32)], axis=0)
        wt = jnp.transpose(w)
        acc = jnp.zeros((D_ATT, SAMPLE_ROWS), F32)
        for n in range(n_past_blocks):
            acc = acc + o_all[n] * wt[n:n + 1, :]
        vnt = vnt_ref[...]
        for t in range(n_tok):
            acc = acc + vnt[:, t:t + 1] * wt[n_past_blocks + t:n_past_blocks + t + 1, :]
        out = jnp.transpose(acc)[:n_rows] * hmask
        o_ref[...] = jnp.sum(out.reshape(n_tok, N_HEADS, D_ATT), axis=1)


def _moba_sample(layer, page_table, q, k_new, v_new_t, cache_kt, cache_vt, slope_rows, tpos_rows, hmask, past_len):
    dec_batch, n_tok, _ = q.shape
    n_pages = page_table.shape[1]
    assert past_len % MOBA_BLOCK == 0 and n_pages * PAGE_SIZE == past_len and n_pages % PAGES_PER_STEP == 0
    n_past_blocks = past_len // MOBA_BLOCK
    n_rows = n_tok * N_HEADS
    assert SAMPLE_ROWS == LANES and n_rows <= SAMPLE_ROWS and 0 < n_past_blocks <= LANES - n_tok
    assert n_rows % 16 == 0
    n_steps = n_pages // PAGES_PER_STEP

    def tokspec():
        return pl.BlockSpec((None, n_tok, D_ATT), lambda b, j, pt: (b, 0, 0))

    def const(shape):
        return pl.BlockSpec(shape, lambda b, j, pt: (0,) * len(shape))

    def pagespec(i):
        return pl.BlockSpec((None, None, D_ATT, PAGE_SIZE),
                            lambda b, j, pt: (layer, pt[b * n_pages + j * PAGES_PER_STEP + i], 0, 0))

    grid_spec = pltpu.PrefetchScalarGridSpec(
        num_scalar_prefetch=1,
        grid=(dec_batch, n_steps),
        in_specs=[tokspec(), tokspec(), pl.BlockSpec((None, D_ATT, n_tok), lambda b, j, pt: (b, 0, 0)),
                  const((n_rows, 1)), const((n_rows, 1)), const((n_rows, D_ATT))]
        + [pagespec(i) for i in range(PAGES_PER_STEP)] * 2,
        out_specs=tokspec(),
        scratch_shapes=[pltpu.VMEM((n_past_blocks, D_ATT, SAMPLE_ROWS), F32),
                        pltpu.VMEM((n_rows, LANES), F32),
                        pltpu.VMEM((n_rows, LANES), F32),
                        pltpu.VMEM((n_rows, LANES), F32),
                        pltpu.VMEM((BLOCKS_PER_STEP, n_rows, MOBA_BLOCK), F32),
                        pltpu.VMEM((BLOCKS_PER_STEP, SAMPLE_ROWS, MOBA_BLOCK), BF16)],
    )
    return pl.pallas_call(
        functools.partial(_moba_sample_kernel, n_past_blocks, float(past_len)),
        grid_spec=grid_spec,
        out_shape=jax.ShapeDtypeStruct((dec_batch, n_tok, D_ATT), F32),
        compiler_params=pltpu.CompilerParams(dimension_semantics=("arbitrary",) * 2, vmem_limit_bytes=VMEM_LIMIT),
        name="moba_sample",
    )(page_table.reshape(-1), q, k_new, v_new_t, slope_rows, tpos_rows, hmask,
      *([cache_kt] * PAGES_PER_STEP), *([cache_vt] * PAGES_PER_STEP))


def _rglru_coeffs(xc, wa_ref, ba_ref, wx_ref, bx_ref, lam_ref):
    xb = xc.astype(BF16)

    def block_diag(w_ref):
        return jnp.concatenate(
            [jnp.dot(xb[:, n * RNN_BLOCK:(n + 1) * RNN_BLOCK], w_ref[n], preferred_element_type=F32)
             for n in range(N_RNN_BLOCKS)], axis=-1)

    r = jax.nn.sigmoid(block_diag(wa_ref) + ba_ref[...])
    gate_i = jax.nn.sigmoid(block_diag(wx_ref) + bx_ref[...])
    z = -lam_ref[...]
    softplus = jnp.maximum(z, 0.0) + jnp.log1p(jnp.exp(-jnp.abs(z)))
    log_a = -LRU_C * r * softplus
    a = jnp.exp(log_a)
    b = jnp.sqrt(1.0 - a * a) * (gate_i * xc)
    return a, b


def _rglru_prompt_kernel(x_ref, cw_ref, cb_ref, wa_ref, ba_ref, wx_ref, bx_ref, lam_ref,
                         y_ref, conv_ref, hlast_ref, xtail, hcarry):
    t = pl.program_id(1)
    tt = x_ref.shape[0]

    @pl.when(t == 0)
    def _():
        xtail[...] = jnp.zeros(xtail.shape, F32)
        hcarry[...] = jnp.zeros(hcarry.shape, F32)

    n_grp = tt // SCAN_GROUP
    x = x_ref[...].reshape(n_grp, SCAN_GROUP, D_RNN)
    tail = xtail[...]
    row = lax.broadcasted_iota(jnp.int32, (n_grp, SCAN_GROUP, D_RNN), 1)
    xc = cb_ref[...] + x * cw_ref[CONV_W - 1:CONV_W, :]
    for d in range(1, CONV_W):
        rot = pltpu.roll(x, d, 1)
        before = jnp.concatenate([pltpu.roll(tail, d, 0)[None], rot[:-1]], axis=0)
        xs = jnp.where(row < d, before, rot)
        xc = xc + xs * cw_ref[CONV_W - 1 - d:CONV_W - d, :]
    a, b = _rglru_coeffs(xc.reshape(tt, D_RNN), wa_ref, ba_ref, wx_ref, bx_ref, lam_ref)
    a = a.reshape(n_grp, SCAN_GROUP, D_RNN)
    b = b.reshape(n_grp, SCAN_GROUP, D_RNN)
    d = 1
    while d < SCAN_GROUP:
        keep = row >= d
        b = jnp.where(keep, a * pltpu.roll(b, d, 1) + b, b)
        a = jnp.where(keep, a * pltpu.roll(a, d, 1), a)
        d *= 2
    carry = hcarry[...]
    for g2 in range(n_grp // 2):
        h_pair = []
        for g in (2 * g2, 2 * g2 + 1):
            h_pair.append(a[g] * carry + b[g])
            carry = h_pair[-1][SCAN_GROUP - 1:SCAN_GROUP]
        y_ref[2 * g2 * SCAN_GROUP:2 * (g2 + 1) * SCAN_GROUP, :] = jnp.concatenate(h_pair, axis=0).astype(y_ref.dtype)
    xtail[...] = x[n_grp - 1]
    hcarry[...] = carry

    @pl.when(t == pl.num_programs(1) - 1)
    def _():
        conv_ref[...] = x_ref[tt - (CONV_W - 1):tt, :]
        hlast_ref[...] = carry


def _rglru_prompt(xr, conv_w, conv_b, wa_b, b_a, wx_b, b_x, lam):
    batch, seq, _ = xr.shape
    tt = 256
    assert seq % tt == 0 and tt % (2 * SCAN_GROUP) == 0

    def const(shape):
        return pl.BlockSpec(shape, lambda b, t: (0,) * len(shape))

    wspec = const((N_RNN_BLOCKS, RNN_BLOCK, RNN_BLOCK))
    vec = const((1, D_RNN))
    return pl.pallas_call(
        _rglru_prompt_kernel,
        grid=(batch, seq // tt),
        in_specs=[pl.BlockSpec((None, tt, D_RNN), lambda b, t: (b, t, 0)),
                  const((CONV_W, D_RNN)), vec, wspec, vec, wspec, vec, vec],
        out_specs=[pl.BlockSpec((None, tt, D_RNN), lambda b, t: (b, t, 0)),
                   pl.BlockSpec((None, CONV_W - 1, D_RNN), lambda b, t: (b, 0, 0)),
                   pl.BlockSpec((None, 1, D_RNN), lambda b, t: (b, 0, 0))],
        out_shape=[jax.ShapeDtypeStruct((batch, seq, D_RNN), BF16),
                   jax.ShapeDtypeStruct((batch, CONV_W - 1, D_RNN), F32),
                   jax.ShapeDtypeStruct((batch, 1, D_RNN), F32)],
        scratch_shapes=[pltpu.VMEM((8, D_RNN), F32), pltpu.VMEM((1, D_RNN), F32)],
        compiler_params=pltpu.CompilerParams(dimension_semantics=("arbitrary",) * 2, vmem_limit_bytes=VMEM_LIMIT),
        name="rglru_prompt",
    )(xr, conv_w, conv_b, wa_b, b_a, wx_b, b_x, lam)


def _rglru_sample_kernel(x_ref, prev_ref, h0_ref, cw_ref, cb_ref, wa_ref, ba_ref, wx_ref, bx_ref, lam_ref,
                         y_ref, hlast_ref):
    n_tok, n_seq = x_ref.shape[0], x_ref.shape[1]
    xp = [prev_ref[i] for i in range(CONV_W - 1)] + [x_ref[i] for i in range(n_tok)]
    xc = [cb_ref[...] + functools.reduce(lambda u, w: u + w, [xp[t + j] * cw_ref[j:j + 1, :] for j in range(CONV_W)])
          for t in range(n_tok)]
    a, b = _rglru_coeffs(jnp.concatenate(xc, axis=0), wa_ref, ba_ref, wx_ref, bx_ref, lam_ref)
    h = h0_ref[...]
    for t in range(n_tok):
        h = a[t * n_seq:(t + 1) * n_seq] * h + b[t * n_seq:(t + 1) * n_seq]
        y_ref[t] = h
    hlast_ref[...] = h


def _rglru_sample(xr_t, prev_t, h0, conv_w, conv_b, wa_b, b_a, wx_b, b_x, lam):
    n_tok, n_seq, _ = xr_t.shape
    assert n_seq % 8 == 0
    return pl.pallas_call(
        _rglru_sample_kernel,
        out_shape=[jax.ShapeDtypeStruct((n_tok, n_seq, D_RNN), F32), jax.ShapeDtypeStruct((n_seq, D_RNN), F32)],
        compiler_params=pltpu.CompilerParams(vmem_limit_bytes=VMEM_LIMIT),
        name="rglru_sample",
    )(xr_t, prev_t, h0, conv_w, conv_b, wa_b, b_a, wx_b, b_x, lam)


def _merge_kernel(x_ref, att_ref, ga_ref, rnn_ref, gr_ref, ma_ref, mr_ref, wao_ref, wro_ref, wo_ref, y_ref):
    def f32(ref):
        return ref[...].astype(F32)

    y_a = jnp.dot((f32(att_ref) * jax.nn.silu(f32(ga_ref))).astype(BF16), wao_ref[...], preferred_element_type=F32)
    y_r = jnp.dot((f32(rnn_ref) * jax.nn.silu(f32(gr_ref))).astype(BF16), wro_ref[...], preferred_element_type=F32)
    z = jax.nn.sigmoid(f32(ma_ref)) * y_a + jax.nn.sigmoid(f32(mr_ref)) * y_r
    y_ref[...] = x_ref[...] + jnp.dot(z.astype(BF16), wo_ref[...], preferred_element_type=F32)


def _merge(x, att, ga, rnn, gr, ma, mr, wao_b, wro_b, wo_b):
    rows = x.shape[0]
    tm = min(rows, 256)
    assert rows % tm == 0

    def rowspec(width):
        return pl.BlockSpec((tm, width), lambda i: (i, 0))

    def const(shape):
        return pl.BlockSpec(shape, lambda i: (0,) * len(shape))

    return pl.pallas_call(
        _merge_kernel,
        grid=(rows // tm,),
        in_specs=[rowspec(D_MODEL), rowspec(D_ATT), rowspec(D_ATT), rowspec(D_RNN), rowspec(D_RNN),
                  rowspec(D_MODEL), rowspec(D_MODEL),
                  const((D_ATT, D_MODEL)), const((D_RNN, D_MODEL)), const((D_MODEL, D_MODEL))],
        out_specs=rowspec(D_MODEL),
        out_shape=jax.ShapeDtypeStruct((rows, D_MODEL), F32),
        compiler_params=pltpu.CompilerParams(dimension_semantics=("arbitrary",), vmem_limit_bytes=VMEM_LIMIT),
        name="merge",
    )(x, att, ga, rnn, gr, ma, mr, wao_b, wro_b, wo_b)


def kernel(x_prompt, x_sample, cache_k, cache_v, state_conv, state_h, page_table, norm_w, w_in, q_norm_w, k_norm_w,
           conv_w, conv_b, w_gate_a, b_gate_a, w_gate_x, b_gate_x, lru_lambda, w_attn_out, w_rnn_out, w_out):
    batch, seq, _ = x_prompt.shape
    dec_batch, dec_seq, _ = x_sample.shape
    depth = w_in.shape[0]
    n_pool = cache_k.shape[1]
    past_len = page_table.shape[1] * PAGE_SIZE
    n_blocks = seq // MOBA_BLOCK

    slopes = jnp.asarray([2.0 ** (-8.0 * (h + 1) / N_HEADS) for h in range(N_HEADS)], F32)
    head_of_lane = np.arange(D_ATT) // HEAD_DIM
    gmean = jnp.asarray((head_of_lane[:, None] == head_of_lane[None, :]) / HEAD_DIM, BF16)
    row_head = np.arange(dec_seq * N_HEADS) % N_HEADS
    slope_rows = jnp.asarray(2.0 ** (-8.0 * (row_head + 1) / N_HEADS), F32)[:, None]
    tpos_rows = jnp.asarray(np.arange(dec_seq * N_HEADS) // N_HEADS, F32)[:, None]
    hmask = jnp.asarray(row_head[:, None] == head_of_lane[None, :], F32)

    w_in_b = w_in.astype(BF16)
    wa_b = w_gate_a.astype(BF16)
    wx_b = w_gate_x.astype(BF16)
    wao_b = w_attn_out.astype(BF16)
    wro_b = w_rnn_out.astype(BF16)
    wo_b = w_out.astype(BF16)
    cache_kt = jnp.transpose(cache_k, (0, 1, 3, 4, 2)).reshape(depth, n_pool, D_ATT, PAGE_SIZE)
    cache_vt = jnp.transpose(cache_v, (0, 1, 3, 4, 2)).reshape(depth, n_pool, D_ATT, PAGE_SIZE)

    y_p = x_prompt.reshape(batch * seq, D_MODEL)
    y_s = x_sample.reshape(dec_batch * dec_seq, D_MODEL)
    outs = {name: [] for name in ("cp", "hp", "ks", "vs", "cs", "hs")}
    kt_all = vt_all = None
    for l in range(depth):
        nw = norm_w[l][None, :]
        qn_t = jnp.tile(q_norm_w[l], N_HEADS)[None, :]
        kn_t = jnp.tile(k_norm_w[l], N_HEADS)[None, :]
        rnn_w = (conv_w[l], conv_b[l][None, :], wa_b[l], b_gate_a[l][None, :], wx_b[l], b_gate_x[l][None, :],
                 lru_lambda[l][None, :])
        out_w = (wao_b[l], wro_b[l], wo_b[l])

        q, kt_all, vt_all, ka, va, kmean, ga, xr, gr, ma, mr = _proj_prompt(
            y_p, batch, l, depth, None if l == 0 else (kt_all, vt_all), nw, w_in_b[l], qn_t, kn_t, gmean)
        att = _moba_prompt(slopes, q.reshape(batch, seq, D_ATT),
                           ka.reshape(batch, n_blocks, N_HEADS, 2 * HEAD_DIM, MOBA_BLOCK), va,
                           kmean.reshape(batch, n_blocks, D_ATT))
        rnn, c_new, h_new = _rglru_prompt(xr.reshape(batch, seq, D_RNN), *rnn_w)
        y_p = _merge(y_p, att.reshape(batch * seq, D_ATT), ga, rnn.reshape(batch * seq, D_RNN), gr, ma, mr, *out_w)
        outs["cp"].append(c_new)
        outs["hp"].append(h_new.reshape(batch, D_RNN))

        q, k, v, ga, xr, gr, ma, mr = _proj_sample(y_s, nw, w_in_b[l], qn_t, kn_t, gmean)
        att = _moba_sample(l, page_table, q.reshape(dec_batch, dec_seq, D_ATT), k.reshape(dec_batch, dec_seq, D_ATT),
                           jnp.swapaxes(v.reshape(dec_batch, dec_seq, D_ATT), 1, 2), cache_kt, cache_vt,
                           slope_rows, tpos_rows, hmask, past_len)
        xr_t = jnp.swapaxes(xr.reshape(dec_batch, dec_seq, D_RNN), 0, 1)
        prev_t = jnp.swapaxes(state_conv[l], 0, 1)
        rnn_t, h_new = _rglru_sample(xr_t, prev_t, state_h[l], *rnn_w)
        rnn = jnp.swapaxes(rnn_t, 0, 1).reshape(dec_batch * dec_seq, D_RNN)
        c_new = jnp.swapaxes(jnp.concatenate([prev_t, xr_t], axis=0)[dec_seq:], 0, 1)
        y_s = _merge(y_s, att.reshape(dec_batch * dec_seq, D_ATT), ga, rnn, gr, ma, mr, *out_w)
        outs["ks"].append(k.reshape(dec_batch, dec_seq, N_HEADS, HEAD_DIM))
        outs["vs"].append(v.reshape(dec_batch, dec_seq, N_HEADS, HEAD_DIM))
        outs["cs"].append(c_new)
        outs["hs"].append(h_new)

    def from_transposed(t):
        return jnp.transpose(t.reshape(depth, batch, N_HEADS, HEAD_DIM, seq), (0, 1, 4, 2, 3))

    return (y_p.reshape(batch, seq, D_MODEL), y_s.reshape(dec_batch, dec_seq, D_MODEL),
            from_transposed(kt_all), from_transposed(vt_all), jnp.stack(outs["cp"]), jnp.stack(outs["hp"]),
            jnp.stack(outs["ks"]), jnp.stack(outs["vs"]), jnp.stack(outs["cs"]), jnp.stack(outs["hs"]))
```

```python
import functools
import math

import numpy as np
import jax
import jax.numpy as jnp
from jax import lax
from jax.experimental import pallas as pl
from jax.experimental.pallas import tpu as pltpu

F32 = jnp.float32
BF16 = jnp.bfloat16

D_MODEL = 1024
N_HEADS = 8
HEAD_DIM = 64
D_ATT = N_HEADS * HEAD_DIM
D_RNN = D_MODEL
N_RNN_BLOCKS = 8
RNN_BLOCK = D_RNN // N_RNN_BLOCKS
CONV_W = 4
LRU_C = 8.0
MOBA_BLOCK = 256
MOBA_TOPK = 3
PAGE_SIZE = 128
RMS_EPS = 1e-6
SPLITS = (D_ATT, D_ATT, D_ATT, D_ATT, D_RNN, D_RNN, D_MODEL, D_MODEL)
N_IN = sum(SPLITS)
CUTS = tuple(int(c) for c in np.cumsum((0,) + SPLITS))

LANES = 128
HEADS_PER_SLAB = LANES // HEAD_DIM
N_SLABS = D_ATT // LANES
MASK_BIAS = -1e30
VMEM_LIMIT = 56 * 1024 * 1024

PAGES_PER_STEP = 16
BLOCKS_PER_STEP = PAGES_PER_STEP * PAGE_SIZE // MOBA_BLOCK
PAGES_PER_BLOCK = MOBA_BLOCK // PAGE_SIZE

AUG_ROW_ONES_I = 0
AUG_ROW_ONES_QB = 1
AUG_ROW_KEY = 2
AUG_ROW_BLOCK = 3
AUG_ROWS = 4
ROW_CHUNK = 64
HEADS_PER_STEP = 8
PAIRS_PER_TRIP = 2
SCAN_GROUP = 8
assert all(math.frexp(2.0 ** (-8.0 * (h + 1) / N_HEADS))[0] == 0.5 for h in range(N_HEADS)) and MOBA_BLOCK <= 256


def _hilo(x):
    hi = x.astype(BF16)
    lo = (x - hi.astype(F32)).astype(BF16)
    return hi, lo


def _dot_nt(a, b):
    return lax.dot_general(a, b, (((1,), (1,)), ((), ())), preferred_element_type=F32)


def _top_blocks(gate, col, n_sel):
    sel = jnp.zeros(gate.shape, jnp.bool_)
    colf = col.astype(F32)
    for _ in range(n_sel):
        m = jnp.max(gate, axis=-1, keepdims=True)
        idx = jnp.min(jnp.where(gate == m, colf, float(gate.shape[-1])), axis=-1, keepdims=True)
        hit = colf == idx
        sel = jnp.logical_or(sel, jnp.logical_and(hit, m > -jnp.inf))
        gate = jnp.where(hit, -jnp.inf, gate)
    return sel


def _proj_parts(x_ref, nw_ref, w_ref, qn_ref, kn_ref, gmean_ref):
    x = x_ref[...]
    ms = jnp.mean(x * x, axis=-1, keepdims=True)
    h = (x * lax.rsqrt(ms + RMS_EPS) * nw_ref[...]).astype(BF16)

    def part(i):
        return jnp.dot(h, w_ref[:, CUTS[i]:CUTS[i + 1]], preferred_element_type=F32)

    zq, zk = part(0), part(1)
    rows = zq.shape[0]
    sq = jnp.concatenate(_hilo(zq * zq) + _hilo(zk * zk), axis=0)
    ms_h = jnp.dot(sq, gmean_ref[...], preferred_element_type=F32)
    ms_q = ms_h[:rows] + ms_h[rows:2 * rows]
    ms_k = ms_h[2 * rows:3 * rows] + ms_h[3 * rows:]
    q = zq * lax.rsqrt(ms_q + RMS_EPS) * qn_ref[...] * (HEAD_DIM ** -0.5)
    k = zk * lax.rsqrt(ms_k + RMS_EPS) * kn_ref[...]
    return q, k, part


def _proj_sample_kernel(x_ref, nw_ref, w_ref, qn_ref, kn_ref, gmean_ref,
                        q_ref, k_ref, v_ref, ga_ref, xr_ref, gr_ref, ma_ref, mr_ref):
    q, k, part = _proj_parts(x_ref, nw_ref, w_ref, qn_ref, kn_ref, gmean_ref)
    q_ref[...] = q
    k_ref[...] = k
    for i, ref in zip(range(2, 8), (v_ref, ga_ref, xr_ref, gr_ref, ma_ref, mr_ref)):
        ref[...] = part(i)


def _proj_prompt_kernel(n_blocks, n_aliased, x_ref, nw_ref, w_ref, qn_ref, kn_ref, gmean_ref, *rest):
    q_ref, kt_ref, vt_ref, ka_ref, va_ref, km_ref, ga_ref, xr_ref, gr_ref, ma_ref, mr_ref = rest[n_aliased:]
    q, k, part = _proj_parts(x_ref, nw_ref, w_ref, qn_ref, kn_ref, gmean_ref)
    v = part(2)
    q_ref[...] = q
    for i, ref in zip(range(3, 8), (ga_ref, xr_ref, gr_ref, ma_ref, mr_ref)):
        ref[...] = part(i).astype(ref.dtype)
    km_ref[...] = jnp.mean(k, axis=0, keepdims=True)
    kt = jnp.transpose(k)
    kt_ref[...] = kt
    vt_ref[...] = jnp.transpose(v)
    n = pl.program_id(0) % n_blocks
    r = lax.broadcasted_iota(jnp.int32, (HEAD_DIM, MOBA_BLOCK), 0)
    j = lax.broadcasted_iota(jnp.int32, (HEAD_DIM, MOBA_BLOCK), 1).astype(F32)
    ones_rows = (r == n) | (r == n_blocks + AUG_ROW_ONES_I) | (r == n_blocks + AUG_ROW_ONES_QB)
    lane = lax.broadcasted_iota(jnp.int32, (MOBA_BLOCK, LANES), 1)
    for h in range(N_HEADS):
        slope = 2.0 ** (-8.0 * (h + 1) / N_HEADS)
        extra = jnp.where(ones_rows, 1.0, 0.0)
        extra = jnp.where(r == n_blocks + AUG_ROW_KEY, slope * j, extra)
        extra = jnp.where(r == n_blocks + AUG_ROW_BLOCK, slope * (n * MOBA_BLOCK).astype(F32), extra)
        ka_ref[h] = jnp.concatenate([kt[h * HEAD_DIM:(h + 1) * HEAD_DIM].astype(BF16), extra.astype(BF16)], axis=0)
        slab = v[:, (h // HEADS_PER_SLAB) * LANES:(h // HEADS_PER_SLAB + 1) * LANES]
        own = (lane // HEAD_DIM) == (h % HEADS_PER_SLAB)
        va_ref[h] = jnp.where(own, slab, 1.0).astype(BF16)


def _proj_in_specs(tm):
    def const(shape):
        return pl.BlockSpec(shape, lambda i: (0,) * len(shape))

    return [pl.BlockSpec((tm, D_MODEL), lambda i: (i, 0)), const((1, D_MODEL)), const((D_MODEL, N_IN)),
            const((1, D_ATT)), const((1, D_ATT)), const((D_ATT, D_ATT))]


def _proj_sample(x, norm_w, w_in_b, qn_t, kn_t, gmean):
    rows = x.shape[0]
    widths = (D_ATT, D_ATT, D_ATT, D_ATT, D_RNN, D_RNN, D_MODEL, D_MODEL)
    return pl.pallas_call(
        _proj_sample_kernel,
        grid=(1,),
        in_specs=_proj_in_specs(rows),
        out_specs=[pl.BlockSpec((rows, w), lambda i: (i, 0)) for w in widths],
        out_shape=[jax.ShapeDtypeStruct((rows, w), F32) for w in widths],
        compiler_params=pltpu.CompilerParams(dimension_semantics=("arbitrary",), vmem_limit_bytes=VMEM_LIMIT),
        name="proj_sample",
    )(x, norm_w, w_in_b, qn_t, kn_t, gmean)


def _proj_prompt(x, batch, layer, depth, kv_all, norm_w, w_in_b, qn_t, kn_t, gmean):
    rows = x.shape[0]
    seq = rows // batch
    tm = MOBA_BLOCK
    n_blocks = seq // tm
    assert seq % tm == 0 and n_blocks + AUG_ROWS <= HEAD_DIM
    kv_all = () if kv_all is None else tuple(kv_all)
    n_fixed = len(_proj_in_specs(tm))

    def rowspec(width):
        return pl.BlockSpec((tm, width), lambda i: (i, 0))

    def tspec():
        return pl.BlockSpec((None, None, D_ATT, tm), lambda i: (layer, i // n_blocks, 0, i % n_blocks))

    widths = (D_ATT, D_RNN, D_RNN, D_MODEL, D_MODEL)
    out_specs = [rowspec(D_ATT), tspec(), tspec(),
                 pl.BlockSpec((None, N_HEADS, 2 * HEAD_DIM, tm), lambda i: (i, 0, 0, 0)),
                 pl.BlockSpec((None, N_HEADS, tm, LANES), lambda i: (i // n_blocks, 0, i % n_blocks, 0)),
                 pl.BlockSpec((None, 1, D_ATT), lambda i: (i, 0, 0))] + [rowspec(w) for w in widths]
    out_shape = [jax.ShapeDtypeStruct((rows, D_ATT), F32),
                 jax.ShapeDtypeStruct((depth, batch, D_ATT, seq), F32),
                 jax.ShapeDtypeStruct((depth, batch, D_ATT, seq), F32),
                 jax.ShapeDtypeStruct((batch * n_blocks, N_HEADS, 2 * HEAD_DIM, tm), BF16),
                 jax.ShapeDtypeStruct((batch, N_HEADS, seq, LANES), BF16),
                 jax.ShapeDtypeStruct((batch * n_blocks, 1, D_ATT), F32)] + [
                     jax.ShapeDtypeStruct((rows, w), d) for w, d in zip(widths, (BF16, F32, BF16, BF16, BF16))]
    return pl.pallas_call(
        functools.partial(_proj_prompt_kernel, n_blocks, len(kv_all)),
        grid=(rows // tm,),
        in_specs=_proj_in_specs(tm) + [pl.BlockSpec(memory_space=pl.ANY)] * len(kv_all),
        out_specs=out_specs,
        out_shape=out_shape,
        input_output_aliases={n_fixed + i: 1 + i for i in range(len(kv_all))},
        compiler_params=pltpu.CompilerParams(dimension_semantics=("arbitrary",), vmem_limit_bytes=VMEM_LIMIT),
        name="proj_prompt",
    )(x, norm_w, w_in_b, qn_t, kn_t, gmean, *kv_all)


def _top_blocks_t(gate, n_sel):
    sel = jnp.zeros(gate.shape, jnp.bool_)
    blk = lax.broadcasted_iota(jnp.int32, gate.shape, gate.ndim - 2).astype(F32)
    for _ in range(n_sel):
        m = jnp.max(gate, axis=-2, keepdims=True)
        idx = jnp.min(jnp.where(gate == m, blk, float(gate.shape[-2])), axis=-2, keepdims=True)
        hit = blk == idx
        sel = jnp.logical_or(sel, jnp.logical_and(hit, m > -jnp.inf))
        gate = jnp.where(hit, -jnp.inf, gate)
    return sel


def _moba_prompt_kernel(slopes_ref, q_ref, ka_ref, va_ref, km_ref, o_ref,
                        qa_past_ref, qa_own_ref, s_ref, p_ref, m_ref, acc_ref):
    qb = pl.program_id(2)
    n_blocks = km_ref.shape[0]
    tq = MOBA_BLOCK
    lane = lax.broadcasted_iota(jnp.int32, (tq, LANES), 1)
    lane_km = lax.broadcasted_iota(jnp.int32, (n_blocks, HEADS_PER_STEP * HEAD_DIM), 1)
    xrow = lax.broadcasted_iota(jnp.int32, (HEAD_DIM - n_blocks, tq), 0)
    qpos = lax.broadcasted_iota(jnp.int32, (HEAD_DIM - n_blocks, tq), 1).astype(F32)
    q = q_ref[...]
    q_hi, q_lo = _hilo(q)
    km = km_ref[...]
    qb_f = (qb * tq).astype(F32)

    def head_dims(hh):
        q_pair = q[:, (hh // 2) * LANES:(hh // 2 + 1) * LANES]
        return q_pair if hh % 2 == 0 else pltpu.roll(q_pair, HEAD_DIM, 1)

    aug = lane - HEAD_DIM - n_blocks
    rowi = lax.broadcasted_iota(jnp.int32, (tq, LANES), 0).astype(F32)
    for hh in range(HEADS_PER_STEP):
        slope = slopes_ref[pl.program_id(1) * HEADS_PER_STEP + hh]
        cols = jnp.where(aug == AUG_ROW_ONES_I, -slope * rowi, 0.0)
        cols = jnp.where(aug == AUG_ROW_ONES_QB, -slope * qb_f, cols)
        cols = jnp.where((aug == AUG_ROW_KEY) | (aug == AUG_ROW_BLOCK), 1.0, cols)
        qa_own_ref[hh] = jnp.where(lane < HEAD_DIM, head_dims(hh), cols).astype(BF16)
        m_ref[hh] = jnp.full((tq, LANES), -jnp.inf, F32)
        acc_ref[hh] = jnp.zeros((tq, LANES), F32)

    def select_past_blocks():
        km_heads = jnp.concatenate([jnp.where((lane_km // HEAD_DIM) == hh, km, 0.0) for hh in range(HEADS_PER_STEP)],
                                   axis=0)
        km_hi, km_lo = _hilo(km_heads)
        gate = _dot_nt(km_hi, q_hi) + _dot_nt(km_hi, q_lo) + _dot_nt(km_lo, q_hi)
        gate = gate.reshape(HEADS_PER_STEP, n_blocks, tq)
        blk = lax.broadcasted_iota(jnp.int32, gate.shape, 1)
        sel = _top_blocks_t(jnp.where(blk < qb, gate, -jnp.inf), min(MOBA_TOPK, n_blocks - 1))
        bias_t = jnp.where(sel, 0.0, MASK_BIAS)
        for hh in range(HEADS_PER_STEP):
            slope = slopes_ref[pl.program_id(1) * HEADS_PER_STEP + hh]
            extra_t = jnp.where(xrow == AUG_ROW_ONES_I, -slope * qpos, 0.0)
            extra_t = jnp.where(xrow == AUG_ROW_ONES_QB, -slope * qb_f, extra_t)
            extra_t = jnp.where((xrow == AUG_ROW_KEY) | (xrow == AUG_ROW_BLOCK), 1.0, extra_t)
            cols = jnp.transpose(jnp.concatenate([jnp.zeros((HEAD_DIM, tq), F32), bias_t[hh], extra_t], axis=0))
            qa_past_ref[hh] = jnp.where(lane < HEAD_DIM, head_dims(hh), cols).astype(BF16)

    def update(qa_ref, starts, n_blk, causal):
        tiles = n_blk * tq // LANES

        def scores(n0, hh):
            for b in range(n_blk):
                s_ref[hh, :, b * tq:(b + 1) * tq] = jnp.dot(qa_ref[hh], ka_ref[n0 + b, hh],
                                                            preferred_element_type=F32)

        def softmax(n0, hh):
            for c in range(tq // ROW_CHUNK):
                rows = slice(c * ROW_CHUNK, (c + 1) * ROW_CHUNK)
                m_old = m_ref[hh, rows, :]
                s = [s_ref[hh, rows, t * LANES:(t + 1) * LANES] for t in range(tiles)]
                if causal:
                    key = lax.broadcasted_iota(jnp.int32, (ROW_CHUNK, LANES), 1)
                    qry = lax.broadcasted_iota(jnp.int32, (ROW_CHUNK, LANES), 0) + c * ROW_CHUNK
                    s = [jnp.where(key + t * LANES <= qry, s[t], -jnp.inf) for t in range(tiles)]
                m_new = jnp.maximum(m_old, jnp.max(functools.reduce(jnp.maximum, s), axis=-1, keepdims=True))
                for t in range(tiles):
                    p_ref[hh, rows, t * LANES:(t + 1) * LANES] = jnp.exp(s[t] - m_new).astype(BF16)
                acc_ref[hh, rows, :] = acc_ref[hh, rows, :] * jnp.exp(m_old - m_new)
                m_ref[hh, rows, :] = m_new

        def values(n0, hh):
            off = pl.multiple_of(n0 * tq, tq)
            acc_ref[hh] += jnp.dot(p_ref[hh, :, :n_blk * tq], va_ref[hh, pl.ds(off, n_blk * tq), :],
                                   preferred_element_type=F32)

        units = [(n0, hh) for n0 in starts for hh in range(HEADS_PER_STEP)]
        scores(*units[0])
        for u, unit in enumerate(units):
            if u + 1 < len(units):
                scores(*units[u + 1])
            softmax(*unit)
            values(*unit)

    update(qa_own_ref, [qb], 1, True)
    select_past_blocks()

    n_pairs = (qb + 1) // 2

    def body(i, carry):
        update(qa_past_ref, [2 * (PAIRS_PER_TRIP * i + k) for k in range(PAIRS_PER_TRIP)], 2, False)
        return carry

    lax.fori_loop(0, n_pairs // PAIRS_PER_TRIP, body, 0)
    if PAIRS_PER_TRIP > 1:
        def tail(i, carry):
            update(qa_past_ref, [2 * i], 2, False)
            return carry

        lax.fori_loop((n_pairs // PAIRS_PER_TRIP) * PAIRS_PER_TRIP, n_pairs, tail, 0)

    for hh in range(HEADS_PER_STEP):
        acc = acc_ref[hh]
        o_h = acc / pltpu.roll(acc, HEAD_DIM, 1)
        if hh % 2 == 1:
            pair = jnp.where(lane < HEAD_DIM, o_prev, o_h)
            o_ref[:, (hh // 2) * LANES:(hh // 2 + 1) * LANES] = pair.astype(o_ref.dtype)
        o_prev = o_h


def _moba_prompt(slopes, q, ka, va, kmean):
    batch, seq, _ = q.shape
    n_blocks = seq // MOBA_BLOCK
    assert seq % MOBA_BLOCK == 0 and n_blocks > MOBA_TOPK and n_blocks % 2 == 0
    assert HEADS_PER_SLAB == 2 and HEADS_PER_STEP % 2 == 0 and N_HEADS % HEADS_PER_STEP == 0
    tq = MOBA_BLOCK
    width = HEADS_PER_STEP * HEAD_DIM
    return pl.pallas_call(
        _moba_prompt_kernel,
        grid=(batch, N_HEADS // HEADS_PER_STEP, n_blocks),
        in_specs=[pl.BlockSpec(memory_space=pltpu.SMEM),
                  pl.BlockSpec((None, tq, width), lambda b, g, i: (b, i, g)),
                  pl.BlockSpec((None, n_blocks, HEADS_PER_STEP, 2 * HEAD_DIM, tq), lambda b, g, i: (b, 0, g, 0, 0),
                               pipeline_mode=pl.Buffered(1)),
                  pl.BlockSpec((None, HEADS_PER_STEP, seq, LANES), lambda b, g, i: (b, g, 0, 0),
                               pipeline_mode=pl.Buffered(1)),
                  pl.BlockSpec((None, n_blocks, width), lambda b, g, i: (b, 0, g))],
        out_specs=pl.BlockSpec((None, tq, width), lambda b, g, i: (b, i, g)),
        out_shape=jax.ShapeDtypeStruct((batch, seq, D_ATT), BF16),
        scratch_shapes=[pltpu.VMEM((HEADS_PER_STEP, tq, 2 * HEAD_DIM), BF16),
                        pltpu.VMEM((HEADS_PER_STEP, tq, 2 * HEAD_DIM), BF16),
                        pltpu.VMEM((HEADS_PER_STEP, tq, 2 * tq), F32),
                        pltpu.VMEM((HEADS_PER_STEP, tq, 2 * tq), BF16),
                        pltpu.VMEM((HEADS_PER_STEP, tq, LANES), F32),
                        pltpu.VMEM((HEADS_PER_STEP, tq, LANES), F32)],
        compiler_params=pltpu.CompilerParams(dimension_semantics=("arbitrary",) * 3, vmem_limit_bytes=VMEM_LIMIT),
        name="moba_prompt",
    )(slopes, q, ka, va, kmean)


def _moba_sample_kernel(n_past_blocks, past_len, pt_ref, q_ref, kn_ref, vn_ref, slope_ref, tpos_ref, hmask_ref,
                        *rest):
    k_pages = rest[:PAGES_PER_STEP]
    v_pages = rest[PAGES_PER_STEP:2 * PAGES_PER_STEP]
    o_ref, o_all, m_all, l_all, g_all, s_ref, p_ref = rest[2 * PAGES_PER_STEP:]
    del pt_ref
    j = pl.program_id(1)
    n_tok = q_ref.shape[0]
    n_rows = n_tok * N_HEADS
    hmask = hmask_ref[...]
    slope = slope_ref[...]
    tpos = tpos_ref[...]
    q = q_ref[...]
    qbd = jnp.concatenate([jnp.broadcast_to(q[t:t + 1, :], (N_HEADS, D_ATT)) for t in range(n_tok)], axis=0) * hmask
    q_hilo = jnp.concatenate(_hilo(qbd), axis=0)
    colk = lax.broadcasted_iota(jnp.int32, (n_rows, MOBA_BLOCK), 1).astype(F32)
    colb = lax.broadcasted_iota(jnp.int32, (n_rows, LANES), 1)

    @pl.when(j == 0)
    def _():
        m_all[...] = jnp.full(m_all.shape, -jnp.inf, F32)
        l_all[...] = jnp.zeros(l_all.shape, F32)
        g_all[...] = jnp.full(g_all.shape, -jnp.inf, F32)

    def block_t(page_refs, i):
        pages = range(i * PAGES_PER_BLOCK, (i + 1) * PAGES_PER_BLOCK)
        return jnp.concatenate([page_refs[p][...] for p in pages], axis=1).astype(BF16)

    def scores(i):
        kt = block_t(k_pages, i)
        both = jnp.dot(q_hilo, kt, preferred_element_type=F32)
        s_ref[i] = both[:n_rows] + both[n_rows:]

    def softmax(i):
        n = j * BLOCKS_PER_STEP + i
        raw = s_ref[i]
        gate = jnp.mean(raw, axis=-1, keepdims=True)
        dist = (past_len + tpos - (n * MOBA_BLOCK).astype(F32)) - colk
        s = raw - slope * dist
        m = jnp.max(s, axis=-1, keepdims=True)
        p = jnp.exp(s - m)
        l = jnp.sum(p, axis=-1, keepdims=True)
        p_ref[i] = p.astype(BF16)
        here = colb == n
        m_all[...] = jnp.where(here, m, m_all[...])
        l_all[...] = jnp.where(here, l, l_all[...])
        g_all[...] = jnp.where(here, gate, g_all[...])

    def values(i):
        o_all[j * BLOCKS_PER_STEP + i] = _dot_nt(p_ref[i], block_t(v_pages, i))

    scores(0)
    for i in range(BLOCKS_PER_STEP):
        if i + 1 < BLOCKS_PER_STEP:
            scores(i + 1)
        softmax(i)
        values(i)

    @pl.when(j == pl.num_programs(1) - 1)
    def _():
        sel = _top_blocks(g_all[...], colb, min(MOBA_TOPK, n_past_blocks))
        kn = kn_ref[...]
        s_own = []
        for t in range(n_tok):
            s_t = jnp.sum(qbd * kn[t:t + 1, :], axis=-1, keepdims=True) - slope * (tpos - float(t))
            s_own.append(jnp.where(tpos >= float(t), s_t, -jnp.inf))
        m_own = functools.reduce(jnp.maximum, s_own)
        m_sel = jnp.max(jnp.where(sel, m_all[...], -jnp.inf), axis=-1, keepdims=True)
        m_tot = jnp.maximum(m_own, m_sel)
        w = jnp.where(sel, jnp.exp(m_all[...] - m_tot), 0.0)
        l_tot = jnp.sum(w * l_all[...], axis=-1, keepdims=True)
        vn = vn_ref[...]
        acc = jnp.zeros((n_rows, D_ATT), F32)
        for t in range(n_tok):
            p_t = jnp.exp(s_own[t] - m_tot)
            l_tot = l_tot + p_t
            acc = acc + p_t * vn[t:t + 1, :]
        for n in range(n_past_blocks):
            acc = acc + w[:, n:n + 1] * o_all[n]
        out = acc * hmask / l_tot
        o_ref[...] = jnp.sum(out.reshape(n_tok, N_HEADS, D_ATT), axis=1)


def _moba_sample(layer, page_table, q, k_new, v_new, cache_kt, cache_vt, slope_rows, tpos_rows, hmask, past_len):
    dec_batch, n_tok, _ = q.shape
    n_pages = page_table.shape[1]
    assert past_len % MOBA_BLOCK == 0 and n_pages * PAGE_SIZE == past_len and n_pages % PAGES_PER_STEP == 0
    n_past_blocks = past_len // MOBA_BLOCK
    n_rows = n_tok * N_HEADS
    assert 0 < n_past_blocks <= LANES
    assert n_rows % 16 == 0
    n_steps = n_pages // PAGES_PER_STEP

    def tokspec():
        return pl.BlockSpec((None, n_tok, D_ATT), lambda b, j, pt: (b, 0, 0))

    def const(shape):
        return pl.BlockSpec(shape, lambda b, j, pt: (0,) * len(shape))

    def pagespec(i):
        return pl.BlockSpec((None, None, D_ATT, PAGE_SIZE),
                            lambda b, j, pt: (layer, pt[b * n_pages + j * PAGES_PER_STEP + i], 0, 0))

    grid_spec = pltpu.PrefetchScalarGridSpec(
        num_scalar_prefetch=1,
        grid=(dec_batch, n_steps),
        in_specs=[tokspec(), tokspec(), tokspec(),
                  const((n_rows, 1)), const((n_rows, 1)), const((n_rows, D_ATT))]
        + [pagespec(i) for i in range(PAGES_PER_STEP)] * 2,
        out_specs=tokspec(),
        scratch_shapes=[pltpu.VMEM((n_past_blocks, n_rows, D_ATT), F32),
                        pltpu.VMEM((n_rows, LANES), F32),
                        pltpu.VMEM((n_rows, LANES), F32),
                        pltpu.VMEM((n_rows, LANES), F32),
                        pltpu.VMEM((BLOCKS_PER_STEP, n_rows, MOBA_BLOCK), F32),
                        pltpu.VMEM((BLOCKS_PER_STEP, n_rows, MOBA_BLOCK), BF16)],
    )
    return pl.pallas_call(
        functools.partial(_moba_sample_kernel, n_past_blocks, float(past_len)),
        grid_spec=grid_spec,
        out_shape=jax.ShapeDtypeStruct((dec_batch, n_tok, D_ATT), F32),
        compiler_params=pltpu.CompilerParams(dimension_semantics=("arbitrary",) * 2, vmem_limit_bytes=VMEM_LIMIT),
        name="moba_sample",
    )(page_table.reshape(-1), q, k_new, v_new, slope_rows, tpos_rows, hmask,
      *([cache_kt] * PAGES_PER_STEP), *([cache_vt] * PAGES_PER_STEP))


def _rglru_coeffs(xc, wa_ref, ba_ref, wx_ref, bx_ref, lam_ref):
    xb = xc.astype(BF16)

    def block_diag(w_ref):
        return jnp.concatenate(
            [jnp.dot(xb[:, n * RNN_BLOCK:(n + 1) * RNN_BLOCK], w_ref[n], preferred_element_type=F32)
             for n in range(N_RNN_BLOCKS)], axis=-1)

    r = jax.nn.sigmoid(block_diag(wa_ref) + ba_ref[...])
    gate_i = jax.nn.sigmoid(block_diag(wx_ref) + bx_ref[...])
    z = -lam_ref[...]
    softplus = jnp.maximum(z, 0.0) + jnp.log1p(jnp.exp(-jnp.abs(z)))
    log_a = -LRU_C * r * softplus
    a = jnp.exp(log_a)
    b = jnp.sqrt(1.0 - a * a) * (gate_i * xc)
    return a, b


def _rglru_prompt_kernel(x_ref, cw_ref, cb_ref, wa_ref, ba_ref, wx_ref, bx_ref, lam_ref,
                         y_ref, conv_ref, hlast_ref, xtail, hcarry):
    t = pl.program_id(1)
    tt = x_ref.shape[0]

    @pl.when(t == 0)
    def _():
        xtail[...] = jnp.zeros(xtail.shape, F32)
        hcarry[...] = jnp.zeros(hcarry.shape, F32)

    n_grp = tt // SCAN_GROUP
    x = x_ref[...].reshape(n_grp, SCAN_GROUP, D_RNN)
    tail = xtail[...]
    row = lax.broadcasted_iota(jnp.int32, (n_grp, SCAN_GROUP, D_RNN), 1)
    xc = cb_ref[...] + x * cw_ref[CONV_W - 1:CONV_W, :]
    for d in range(1, CONV_W):
        rot = pltpu.roll(x, d, 1)
        before = jnp.concatenate([pltpu.roll(tail, d, 0)[None], rot[:-1]], axis=0)
        xs = jnp.where(row < d, before, rot)
        xc = xc + xs * cw_ref[CONV_W - 1 - d:CONV_W - d, :]
    a, b = _rglru_coeffs(xc.reshape(tt, D_RNN), wa_ref, ba_ref, wx_ref, bx_ref, lam_ref)
    a = a.reshape(n_grp, SCAN_GROUP, D_RNN)
    b = b.reshape(n_grp, SCAN_GROUP, D_RNN)
    d = 1
    while d < SCAN_GROUP:
        keep = row >= d
        b = jnp.where(keep, a * pltpu.roll(b, d, 1) + b, b)
        a = jnp.where(keep, a * pltpu.roll(a, d, 1), a)
        d *= 2
    carry = hcarry[...]
    for g2 in range(n_grp // 2):
        h_pair = []
        for g in (2 * g2, 2 * g2 + 1):
            h_pair.append(a[g] * carry + b[g])
            carry = h_pair[-1][SCAN_GROUP - 1:SCAN_GROUP]
        y_ref[2 * g2 * SCAN_GROUP:2 * (g2 + 1) * SCAN_GROUP, :] = jnp.concatenate(h_pair, axis=0).astype(y_ref.dtype)
    xtail[...] = x[n_grp - 1]
    hcarry[...] = carry

    @pl.when(t == pl.num_programs(1) - 1)
    def _():
        conv_ref[...] = x_ref[tt - (CONV_W - 1):tt, :]
        hlast_ref[...] = carry


def _rglru_prompt(xr, conv_w, conv_b, wa_b, b_a, wx_b, b_x, lam):
    batch, seq, _ = xr.shape
    tt = 256
    assert seq % tt == 0 and tt % (2 * SCAN_GROUP) == 0

    def const(shape):
        return pl.BlockSpec(shape, lambda b, t: (0,) * len(shape))

    wspec = const((N_RNN_BLOCKS, RNN_BLOCK, RNN_BLOCK))
    vec = const((1, D_RNN))
    return pl.pallas_call(
        _rglru_prompt_kernel,
        grid=(batch, seq // tt),
        in_specs=[pl.BlockSpec((None, tt, D_RNN), lambda b, t: (b, t, 0)),
                  const((CONV_W, D_RNN)), vec, wspec, vec, wspec, vec, vec],
        out_specs=[pl.BlockSpec((None, tt, D_RNN), lambda b, t: (b, t, 0)),
                   pl.BlockSpec((None, CONV_W - 1, D_RNN), lambda b, t: (b, 0, 0)),
                   pl.BlockSpec((None, 1, D_RNN), lambda b, t: (b, 0, 0))],
        out_shape=[jax.ShapeDtypeStruct((batch, seq, D_RNN), BF16),
                   jax.ShapeDtypeStruct((batch, CONV_W - 1, D_RNN), F32),
                   jax.ShapeDtypeStruct((batch, 1, D_RNN), F32)],
        scratch_shapes=[pltpu.VMEM((8, D_RNN), F32), pltpu.VMEM((1, D_RNN), F32)],
        compiler_params=pltpu.CompilerParams(dimension_semantics=("arbitrary",) * 2, vmem_limit_bytes=VMEM_LIMIT),
        name="rglru_prompt",
    )(xr, conv_w, conv_b, wa_b, b_a, wx_b, b_x, lam)


def _rglru_sample_kernel(x_ref, prev_ref, h0_ref, cw_ref, cb_ref, wa_ref, ba_ref, wx_ref, bx_ref, lam_ref,
                         y_ref, hlast_ref):
    n_tok, n_seq = x_ref.shape[0], x_ref.shape[1]
    xp = [prev_ref[i] for i in range(CONV_W - 1)] + [x_ref[i] for i in range(n_tok)]
    xc = [cb_ref[...] + functools.reduce(lambda u, w: u + w, [xp[t + j] * cw_ref[j:j + 1, :] for j in range(CONV_W)])
          for t in range(n_tok)]
    a, b = _rglru_coeffs(jnp.concatenate(xc, axis=0), wa_ref, ba_ref, wx_ref, bx_ref, lam_ref)
    h = h0_ref[...]
    for t in range(n_tok):
        h = a[t * n_seq:(t + 1) * n_seq] * h + b[t * n_seq:(t + 1) * n_seq]
        y_ref[t] = h
    hlast_ref[...] = h


def _rglru_sample(xr_t, prev_t, h0, conv_w, conv_b, wa_b, b_a, wx_b, b_x, lam):
    n_tok, n_seq, _ = xr_t.shape
    assert n_seq % 8 == 0
    return pl.pallas_call(
        _rglru_sample_kernel,
        out_shape=[jax.ShapeDtypeStruct((n_tok, n_seq, D_RNN), F32), jax.ShapeDtypeStruct((n_seq, D_RNN), F32)],
        compiler_params=pltpu.CompilerParams(vmem_limit_bytes=VMEM_LIMIT),
        name="rglru_sample",
    )(xr_t, prev_t, h0, conv_w, conv_b, wa_b, b_a, wx_b, b_x, lam)


def _merge_kernel(x_ref, att_ref, ga_ref, rnn_ref, gr_ref, ma_ref, mr_ref, wao_ref, wro_ref, wo_ref, y_ref):
    def f32(ref):
        return ref[...].astype(F32)

    y_a = jnp.dot((f32(att_ref) * jax.nn.silu(f32(ga_ref))).astype(BF16), wao_ref[...], preferred_element_type=F32)
    y_r = jnp.dot((f32(rnn_ref) * jax.nn.silu(f32(gr_ref))).astype(BF16), wro_ref[...], preferred_element_type=F32)
    z = jax.nn.sigmoid(f32(ma_ref)) * y_a + jax.nn.sigmoid(f32(mr_ref)) * y_r
    y_ref[...] = x_ref[...] + jnp.dot(z.astype(BF16), wo_ref[...], preferred_element_type=F32)


def _merge(x, att, ga, rnn, gr, ma, mr, wao_b, wro_b, wo_b):
    rows = x.shape[0]
    tm = min(rows, 256)
    assert rows % tm == 0

    def rowspec(width):
        return pl.BlockSpec((tm, width), lambda i: (i, 0))

    def const(shape):
        return pl.BlockSpec(shape, lambda i: (0,) * len(shape))

    return pl.pallas_call(
        _merge_kernel,
        grid=(rows // tm,),
        in_specs=[rowspec(D_MODEL), rowspec(D_ATT), rowspec(D_ATT), rowspec(D_RNN), rowspec(D_RNN),
                  rowspec(D_MODEL), rowspec(D_MODEL),
                  const((D_ATT, D_MODEL)), const((D_RNN, D_MODEL)), const((D_MODEL, D_MODEL))],
        out_specs=rowspec(D_MODEL),
        out_shape=jax.ShapeDtypeStruct((rows, D_MODEL), F32),
        compiler_params=pltpu.CompilerParams(dimension_semantics=("arbitrary",), vmem_limit_bytes=VMEM_LIMIT),
        name="merge",
    )(x, att, ga, rnn, gr, ma, mr, wao_b, wro_b, wo_b)


def kernel(x_prompt, x_sample, cache_k, cache_v, state_conv, state_h, page_table, norm_w, w_in, q_norm_w, k_norm_w,
           conv_w, conv_b, w_gate_a, b_gate_a, w_gate_x, b_gate_x, lru_lambda, w_attn_out, w_rnn_out, w_out):
    batch, seq, _ = x_prompt.shape
    dec_batch, dec_seq, _ = x_sample.shape
    depth = w_in.shape[0]
    n_pool = cache_k.shape[1]
    past_len = page_table.shape[1] * PAGE_SIZE
    n_blocks = seq // MOBA_BLOCK

    slopes = jnp.asarray([2.0 ** (-8.0 * (h + 1) / N_HEADS) for h in range(N_HEADS)], F32)
    head_of_lane = np.arange(D_ATT) // HEAD_DIM
    gmean = jnp.asarray((head_of_lane[:, None] == head_of_lane[None, :]) / HEAD_DIM, BF16)
    row_head = np.arange(dec_seq * N_HEADS) % N_HEADS
    slope_rows = jnp.asarray(2.0 ** (-8.0 * (row_head + 1) / N_HEADS), F32)[:, None]
    tpos_rows = jnp.asarray(np.arange(dec_seq * N_HEADS) // N_HEADS, F32)[:, None]
    hmask = jnp.asarray(row_head[:, None] == head_of_lane[None, :], F32)

    w_in_b = w_in.astype(BF16)
    wa_b = w_gate_a.astype(BF16)
    wx_b = w_gate_x.astype(BF16)
    wao_b = w_attn_out.astype(BF16)
    wro_b = w_rnn_out.astype(BF16)
    wo_b = w_out.astype(BF16)
    cache_kt = jnp.transpose(cache_k, (0, 1, 3, 4, 2)).reshape(depth, n_pool, D_ATT, PAGE_SIZE)
    cache_vt = jnp.transpose(cache_v, (0, 1, 3, 4, 2)).reshape(depth, n_pool, D_ATT, PAGE_SIZE)

    y_p = x_prompt.reshape(batch * seq, D_MODEL)
    y_s = x_sample.reshape(dec_batch * dec_seq, D_MODEL)
    outs = {name: [] for name in ("cp", "hp", "ks", "vs", "cs", "hs")}
    kt_all = vt_all = None
    for l in range(depth):
        nw = norm_w[l][None, :]
        qn_t = jnp.tile(q_norm_w[l], N_HEADS)[None, :]
        kn_t = jnp.tile(k_norm_w[l], N_HEADS)[None, :]
        rnn_w = (conv_w[l], conv_b[l][None, :], wa_b[l], b_gate_a[l][None, :], wx_b[l], b_gate_x[l][None, :],
                 lru_lambda[l][None, :])
        out_w = (wao_b[l], wro_b[l], wo_b[l])

        q, kt_all, vt_all, ka, va, kmean, ga, xr, gr, ma, mr = _proj_prompt(
            y_p, batch, l, depth, None if l == 0 else (kt_all, vt_all), nw, w_in_b[l], qn_t, kn_t, gmean)
        att = _moba_prompt(slopes, q.reshape(batch, seq, D_ATT),
                           ka.reshape(batch, n_blocks, N_HEADS, 2 * HEAD_DIM, MOBA_BLOCK), va,
                           kmean.reshape(batch, n_blocks, D_ATT))
        rnn, c_new, h_new = _rglru_prompt(xr.reshape(batch, seq, D_RNN), *rnn_w)
        y_p = _merge(y_p, att.reshape(batch * seq, D_ATT), ga, rnn.reshape(batch * seq, D_RNN), gr, ma, mr, *out_w)
        outs["cp"].append(c_new)
        outs["hp"].append(h_new.reshape(batch, D_RNN))

        q, k, v, ga, xr, gr, ma, mr = _proj_sample(y_s, nw, w_in_b[l], qn_t, kn_t, gmean)
        att = _moba_sample(l, page_table, q.reshape(dec_batch, dec_seq, D_ATT), k.reshape(dec_batch, dec_seq, D_ATT),
                           v.reshape(dec_batch, dec_seq, D_ATT), cache_kt, cache_vt,
                           slope_rows, tpos_rows, hmask, past_len)
        xr_t = jnp.swapaxes(xr.reshape(dec_batch, dec_seq, D_RNN), 0, 1)
        prev_t = jnp.swapaxes(state_conv[l], 0, 1)
        rnn_t, h_new = _rglru_sample(xr_t, prev_t, state_h[l], *rnn_w)
        rnn = jnp.swapaxes(rnn_t, 0, 1).reshape(dec_batch * dec_seq, D_RNN)
        c_new = jnp.swapaxes(jnp.concatenate([prev_t, xr_t], axis=0)[dec_seq:], 0, 1)
        y_s = _merge(y_s, att.reshape(dec_batch * dec_seq, D_ATT), ga, rnn, gr, ma, mr, *out_w)
        outs["ks"].append(k.reshape(dec_batch, dec_seq, N_HEADS, HEAD_DIM))
        outs["vs"].append(v.reshape(dec_batch, dec_seq, N_HEADS, HEAD_DIM))
        outs["cs"].append(c_new)
        outs["hs"].append(h_new)

    def from_transposed(t):
        return jnp.transpose(t.reshape(depth, batch, N_HEADS, HEAD_DIM, seq), (0, 1, 4, 2, 3))

    return (y_p.reshape(batch, seq, D_MODEL), y_s.reshape(dec_batch, dec_seq, D_MODEL),
            from_transposed(kt_all), from_transposed(vt_all), jnp.stack(outs["cp"]), jnp.stack(outs["hp"]),
            jnp.stack(outs["ks"]), jnp.stack(outs["vs"]), jnp.stack(outs["cs"]), jnp.stack(outs["hs"]))
```

```python
import functools
import math

import numpy as np
import jax
import jax.numpy as jnp
from jax import lax
from jax.experimental import pallas as pl
from jax.experimental.pallas import tpu as pltpu

F32 = jnp.float32
BF16 = jnp.bfloat16

D_MODEL = 1024
N_HEADS = 8
HEAD_DIM = 64
D_ATT = N_HEADS * HEAD_DIM
D_RNN = D_MODEL
N_RNN_BLOCKS = 8
RNN_BLOCK = D_RNN // N_RNN_BLOCKS
CONV_W = 4
LRU_C = 8.0
MOBA_BLOCK = 256
MOBA_TOPK = 3
PAGE_SIZE = 128
RMS_EPS = 1e-6
SPLITS = (D_ATT, D_ATT, D_ATT, D_ATT, D_RNN, D_RNN, D_MODEL, D_MODEL)
N_IN = sum(SPLITS)
CUTS = tuple(int(c) for c in np.cumsum((0,) + SPLITS))

LANES = 128
HEADS_PER_SLAB = LANES // HEAD_DIM
N_SLABS = D_ATT // LANES
MASK_BIAS = -1e30
VMEM_LIMIT = 56 * 1024 * 1024

PAGES_PER_STEP = 16
BLOCKS_PER_STEP = PAGES_PER_STEP * PAGE_SIZE // MOBA_BLOCK
PAGES_PER_BLOCK = MOBA_BLOCK // PAGE_SIZE

AUG_ROW_ONES_I = 0
AUG_ROW_ONES_QB = 1
AUG_ROW_KEY = 2
AUG_ROW_BLOCK = 3
AUG_ROWS = 4
ROW_CHUNK = 64
HEADS_PER_STEP = 8
PAIRS_PER_TRIP = 2
SCAN_GROUP = 8
assert all(math.frexp(2.0 ** (-8.0 * (h + 1) / N_HEADS))[0] == 0.5 for h in range(N_HEADS)) and MOBA_BLOCK <= 256


def _hilo(x):
    hi = x.astype(BF16)
    lo = (x - hi.astype(F32)).astype(BF16)
    return hi, lo


def _dot_nt(a, b):
    return lax.dot_general(a, b, (((1,), (1,)), ((), ())), preferred_element_type=F32)


def _top_blocks(gate, col, n_sel):
    sel = jnp.zeros(gate.shape, jnp.bool_)
    colf = col.astype(F32)
    for _ in range(n_sel):
        m = jnp.max(gate, axis=-1, keepdims=True)
        idx = jnp.min(jnp.where(gate == m, colf, float(gate.shape[-1])), axis=-1, keepdims=True)
        hit = colf == idx
        sel = jnp.logical_or(sel, jnp.logical_and(hit, m > -jnp.inf))
        gate = jnp.where(hit, -jnp.inf, gate)
    return sel


def _proj_parts(x_ref, nw_ref, w_ref, qn_ref, kn_ref, gmean_ref):
    x = x_ref[...]
    ms = jnp.mean(x * x, axis=-1, keepdims=True)
    h = (x * lax.rsqrt(ms + RMS_EPS) * nw_ref[...]).astype(BF16)

    def part(i):
        return jnp.dot(h, w_ref[:, CUTS[i]:CUTS[i + 1]], preferred_element_type=F32)

    zq, zk = part(0), part(1)
    rows = zq.shape[0]
    sq = jnp.concatenate(_hilo(zq * zq) + _hilo(zk * zk), axis=0)
    ms_h = jnp.dot(sq, gmean_ref[...], preferred_element_type=F32)
    ms_q = ms_h[:rows] + ms_h[rows:2 * rows]
    ms_k = ms_h[2 * rows:3 * rows] + ms_h[3 * rows:]
    q = zq * lax.rsqrt(ms_q + RMS_EPS) * qn_ref[...] * (HEAD_DIM ** -0.5)
    k = zk * lax.rsqrt(ms_k + RMS_EPS) * kn_ref[...]
    return q, k, part


def _proj_sample_kernel(x_ref, nw_ref, w_ref, qn_ref, kn_ref, gmean_ref,
                        q_ref, k_ref, v_ref, ga_ref, xr_ref, gr_ref, ma_ref, mr_ref):
    q, k, part = _proj_parts(x_ref, nw_ref, w_ref, qn_ref, kn_ref, gmean_ref)
    q_ref[...] = q
    k_ref[...] = k
    for i, ref in zip(range(2, 8), (v_ref, ga_ref, xr_ref, gr_ref, ma_ref, mr_ref)):
        ref[...] = part(i)


def _proj_prompt_kernel(n_blocks, n_aliased, x_ref, nw_ref, w_ref, qn_ref, kn_ref, gmean_ref, *rest):
    q_ref, kt_ref, vt_ref, ka_ref, va_ref, km_ref, ga_ref, xr_ref, gr_ref, ma_ref, mr_ref = rest[n_aliased:]
    q, k, part = _proj_parts(x_ref, nw_ref, w_ref, qn_ref, kn_ref, gmean_ref)
    v = part(2)
    q_ref[...] = q
    for i, ref in zip(range(3, 8), (ga_ref, xr_ref, gr_ref, ma_ref, mr_ref)):
        ref[...] = part(i).astype(ref.dtype)
    km_ref[...] = jnp.mean(k, axis=0, keepdims=True)
    kt = jnp.transpose(k)
    kt_ref[...] = kt
    vt_ref[...] = jnp.transpose(v)
    n = pl.program_id(0) % n_blocks
    r = lax.broadcasted_iota(jnp.int32, (HEAD_DIM, MOBA_BLOCK), 0)
    j = lax.broadcasted_iota(jnp.int32, (HEAD_DIM, MOBA_BLOCK), 1).astype(F32)
    ones_rows = (r == n) | (r == n_blocks + AUG_ROW_ONES_I) | (r == n_blocks + AUG_ROW_ONES_QB)
    lane = lax.broadcasted_iota(jnp.int32, (MOBA_BLOCK, LANES), 1)
    for h in range(N_HEADS):
        slope = 2.0 ** (-8.0 * (h + 1) / N_HEADS)
        extra = jnp.where(ones_rows, 1.0, 0.0)
        extra = jnp.where(r == n_blocks + AUG_ROW_KEY, slope * j, extra)
        extra = jnp.where(r == n_blocks + AUG_ROW_BLOCK, slope * (n * MOBA_BLOCK).astype(F32), extra)
        ka_ref[h] = jnp.concatenate([kt[h * HEAD_DIM:(h + 1) * HEAD_DIM].astype(BF16), extra.astype(BF16)], axis=0)
        slab = v[:, (h // HEADS_PER_SLAB) * LANES:(h // HEADS_PER_SLAB + 1) * LANES]
        own = (lane // HEAD_DIM) == (h % HEADS_PER_SLAB)
        va_ref[h] = jnp.where(own, slab, 1.0).astype(BF16)


def _proj_in_specs(tm):
    def const(shape):
        return pl.BlockSpec(shape, lambda i: (0,) * len(shape))

    return [pl.BlockSpec((tm, D_MODEL), lambda i: (i, 0)), const((1, D_MODEL)), const((D_MODEL, N_IN)),
            const((1, D_ATT)), const((1, D_ATT)), const((D_ATT, D_ATT))]


def _proj_sample(x, norm_w, w_in_b, qn_t, kn_t, gmean):
    rows = x.shape[0]
    widths = (D_ATT, D_ATT, D_ATT, D_ATT, D_RNN, D_RNN, D_MODEL, D_MODEL)
    return pl.pallas_call(
        _proj_sample_kernel,
        grid=(1,),
        in_specs=_proj_in_specs(rows),
        out_specs=[pl.BlockSpec((rows, w), lambda i: (i, 0)) for w in widths],
        out_shape=[jax.ShapeDtypeStruct((rows, w), F32) for w in widths],
        compiler_params=pltpu.CompilerParams(dimension_semantics=("arbitrary",), vmem_limit_bytes=VMEM_LIMIT),
        name="proj_sample",
    )(x, norm_w, w_in_b, qn_t, kn_t, gmean)


def _proj_prompt(x, batch, layer, depth, kv_all, norm_w, w_in_b, qn_t, kn_t, gmean):
    rows = x.shape[0]
    seq = rows // batch
    tm = MOBA_BLOCK
    n_blocks = seq // tm
    assert seq % tm == 0 and n_blocks + AUG_ROWS <= HEAD_DIM
    kv_all = () if kv_all is None else tuple(kv_all)
    n_fixed = len(_proj_in_specs(tm))

    def rowspec(width):
        return pl.BlockSpec((tm, width), lambda i: (i, 0))

    def tspec():
        return pl.BlockSpec((None, None, D_ATT, tm), lambda i: (layer, i // n_blocks, 0, i % n_blocks))

    widths = (D_ATT, D_RNN, D_RNN, D_MODEL, D_MODEL)
    out_specs = [rowspec(D_ATT), tspec(), tspec(),
                 pl.BlockSpec((None, N_HEADS, 2 * HEAD_DIM, tm), lambda i: (i, 0, 0, 0)),
                 pl.BlockSpec((None, N_HEADS, tm, LANES), lambda i: (i // n_blocks, 0, i % n_blocks, 0)),
                 pl.BlockSpec((None, 1, D_ATT), lambda i: (i, 0, 0))] + [rowspec(w) for w in widths]
    out_shape = [jax.ShapeDtypeStruct((rows, D_ATT), F32),
                 jax.ShapeDtypeStruct((depth, batch, D_ATT, seq), F32),
                 jax.ShapeDtypeStruct((depth, batch, D_ATT, seq), F32),
                 jax.ShapeDtypeStruct((batch * n_blocks, N_HEADS, 2 * HEAD_DIM, tm), BF16),
                 jax.ShapeDtypeStruct((batch, N_HEADS, seq, LANES), BF16),
                 jax.ShapeDtypeStruct((batch * n_blocks, 1, D_ATT), F32)] + [
                     jax.ShapeDtypeStruct((rows, w), d) for w, d in zip(widths, (BF16, F32, BF16, BF16, BF16))]
    return pl.pallas_call(
        functools.partial(_proj_prompt_kernel, n_blocks, len(kv_all)),
        grid=(rows // tm,),
        in_specs=_proj_in_specs(tm) + [pl.BlockSpec(memory_space=pl.ANY)] * len(kv_all),
        out_specs=out_specs,
        out_shape=out_shape,
        input_output_aliases={n_fixed + i: 1 + i for i in range(len(kv_all))},
        compiler_params=pltpu.CompilerParams(dimension_semantics=("arbitrary",), vmem_limit_bytes=VMEM_LIMIT),
        name="proj_prompt",
    )(x, norm_w, w_in_b, qn_t, kn_t, gmean, *kv_all)


def _top_blocks_t(gate, n_sel):
    sel = jnp.zeros(gate.shape, jnp.bool_)
    blk = lax.broadcasted_iota(jnp.int32, gate.shape, gate.ndim - 2).astype(F32)
    for _ in range(n_sel):
        m = jnp.max(gate, axis=-2, keepdims=True)
        idx = jnp.min(jnp.where(gate == m, blk, float(gate.shape[-2])), axis=-2, keepdims=True)
        hit = blk == idx
        sel = jnp.logical_or(sel, jnp.logical_and(hit, m > -jnp.inf))
        gate = jnp.where(hit, -jnp.inf, gate)
    return sel


def _moba_prompt_kernel(slopes_ref, q_ref, ka_ref, va_ref, km_ref, o_ref,
                        qa_past_ref, qa_own_ref, s_ref, p_ref, m_ref, acc_ref):
    qb = pl.program_id(2)
    n_blocks = km_ref.shape[0]
    tq = MOBA_BLOCK
    lane = lax.broadcasted_iota(jnp.int32, (tq, LANES), 1)
    lane_km = lax.broadcasted_iota(jnp.int32, (n_blocks, HEADS_PER_STEP * HEAD_DIM), 1)
    xrow = lax.broadcasted_iota(jnp.int32, (HEAD_DIM - n_blocks, tq), 0)
    qpos = lax.broadcasted_iota(jnp.int32, (HEAD_DIM - n_blocks, tq), 1).astype(F32)
    q = q_ref[...]
    q_hi, q_lo = _hilo(q)
    km = km_ref[...]
    qb_f = (qb * tq).astype(F32)

    def head_dims(hh):
        q_pair = q[:, (hh // 2) * LANES:(hh // 2 + 1) * LANES]
        return q_pair if hh % 2 == 0 else pltpu.roll(q_pair, HEAD_DIM, 1)

    aug = lane - HEAD_DIM - n_blocks
    rowi = lax.broadcasted_iota(jnp.int32, (tq, LANES), 0).astype(F32)
    for hh in range(HEADS_PER_STEP):
        slope = slopes_ref[pl.program_id(1) * HEADS_PER_STEP + hh]
        cols = jnp.where(aug == AUG_ROW_ONES_I, -slope * rowi, 0.0)
        cols = jnp.where(aug == AUG_ROW_ONES_QB, -slope * qb_f, cols)
        cols = jnp.where((aug == AUG_ROW_KEY) | (aug == AUG_ROW_BLOCK), 1.0, cols)
        qa_own_ref[hh] = jnp.where(lane < HEAD_DIM, head_dims(hh), cols).astype(BF16)
        m_ref[hh] = jnp.full((tq, LANES), -jnp.inf, F32)
        acc_ref[hh] = jnp.zeros((tq, LANES), F32)

    def select_past_blocks():
        km_heads = jnp.concatenate([jnp.where((lane_km // HEAD_DIM) == hh, km, 0.0) for hh in range(HEADS_PER_STEP)],
                                   axis=0)
        km_hi, km_lo = _hilo(km_heads)
        gate = _dot_nt(km_hi, q_hi) + _dot_nt(km_hi, q_lo) + _dot_nt(km_lo, q_hi)
        gate = gate.reshape(HEADS_PER_STEP, n_blocks, tq)
        blk = lax.broadcasted_iota(jnp.int32, gate.shape, 1)
        sel = _top_blocks_t(jnp.where(blk < qb, gate, -jnp.inf), min(MOBA_TOPK, n_blocks - 1))
        bias_t = jnp.where(sel, 0.0, MASK_BIAS)
        for hh in range(HEADS_PER_STEP):
            slope = slopes_ref[pl.program_id(1) * HEADS_PER_STEP + hh]
            extra_t = jnp.where(xrow == AUG_ROW_ONES_I, -slope * qpos, 0.0)
            extra_t = jnp.where(xrow == AUG_ROW_ONES_QB, -slope * qb_f, extra_t)
            extra_t = jnp.where((xrow == AUG_ROW_KEY) | (xrow == AUG_ROW_BLOCK), 1.0, extra_t)
            cols = jnp.transpose(jnp.concatenate([jnp.zeros((HEAD_DIM, tq), F32), bias_t[hh], extra_t], axis=0))
            qa_past_ref[hh] = jnp.where(lane < HEAD_DIM, head_dims(hh), cols).astype(BF16)

    def update(qa_ref, starts, n_blk, causal):
        tiles = n_blk * tq // LANES

        def scores(n0, hh):
            for b in range(n_blk):
                s_ref[hh, :, b * tq:(b + 1) * tq] = jnp.dot(qa_ref[hh], ka_ref[n0 + b, hh],
                                                            preferred_element_type=F32)

        def softmax(n0, hh):
            for c in range(tq // ROW_CHUNK):
                rows = slice(c * ROW_CHUNK, (c + 1) * ROW_CHUNK)
                m_old = m_ref[hh, rows, :]
                s = [s_ref[hh, rows, t * LANES:(t + 1) * LANES] for t in range(tiles)]
                if causal:
                    key = lax.broadcasted_iota(jnp.int32, (ROW_CHUNK, LANES), 1)
                    qry = lax.broadcasted_iota(jnp.int32, (ROW_CHUNK, LANES), 0) + c * ROW_CHUNK
                    s = [jnp.where(key + t * LANES <= qry, s[t], -jnp.inf) for t in range(tiles)]
                m_new = jnp.maximum(m_old, jnp.max(functools.reduce(jnp.maximum, s), axis=-1, keepdims=True))
                for t in range(tiles):
                    p_ref[hh, rows, t * LANES:(t + 1) * LANES] = jnp.exp(s[t] - m_new).astype(BF16)
                acc_ref[hh, rows, :] = acc_ref[hh, rows, :] * jnp.exp(m_old - m_new)
                m_ref[hh, rows, :] = m_new

        def values(n0, hh):
            off = pl.multiple_of(n0 * tq, tq)
            acc_ref[hh] += jnp.dot(p_ref[hh, :, :n_blk * tq], va_ref[hh, pl.ds(off, n_blk * tq), :],
                                   preferred_element_type=F32)

        units = [(n0, hh) for n0 in starts for hh in range(HEADS_PER_STEP)]
        scores(*units[0])
        for u, unit in enumerate(units):
            if u + 1 < len(units):
                scores(*units[u + 1])
            softmax(*unit)
            values(*unit)

    update(qa_own_ref, [qb], 1, True)
    select_past_blocks()

    n_pairs = (qb + 1) // 2

    def body(i, carry):
        update(qa_past_ref, [2 * (PAIRS_PER_TRIP * i + k) for k in range(PAIRS_PER_TRIP)], 2, False)
        return carry

    lax.fori_loop(0, n_pairs // PAIRS_PER_TRIP, body, 0)
    if PAIRS_PER_TRIP > 1:
        def tail(i, carry):
            update(qa_past_ref, [2 * i], 2, False)
            return carry

        lax.fori_loop((n_pairs // PAIRS_PER_TRIP) * PAIRS_PER_TRIP, n_pairs, tail, 0)

    for hh in range(HEADS_PER_STEP):
        acc = acc_ref[hh]
        o_h = acc / pltpu.roll(acc, HEAD_DIM, 1)
        if hh % 2 == 1:
            pair = jnp.where(lane < HEAD_DIM, o_prev, o_h)
            o_ref[:, (hh // 2) * LANES:(hh // 2 + 1) * LANES] = pair.astype(o_ref.dtype)
        o_prev = o_h


def _moba_prompt(slopes, q, ka, va, kmean):
    batch, seq, _ = q.shape
    n_blocks = seq // MOBA_BLOCK
    assert seq % MOBA_BLOCK == 0 and n_blocks > MOBA_TOPK and n_blocks % 2 == 0
    assert HEADS_PER_SLAB == 2 and HEADS_PER_STEP % 2 == 0 and N_HEADS % HEADS_PER_STEP == 0
    tq = MOBA_BLOCK
    width = HEADS_PER_STEP * HEAD_DIM
    return pl.pallas_call(
        _moba_prompt_kernel,
        grid=(batch, N_HEADS // HEADS_PER_STEP, n_blocks),
        in_specs=[pl.BlockSpec(memory_space=pltpu.SMEM),
                  pl.BlockSpec((None, tq, width), lambda b, g, i: (b, i, g)),
                  pl.BlockSpec((None, n_blocks, HEADS_PER_STEP, 2 * HEAD_DIM, tq), lambda b, g, i: (b, 0, g, 0, 0),
                               pipeline_mode=pl.Buffered(1)),
                  pl.BlockSpec((None, HEADS_PER_STEP, seq, LANES), lambda b, g, i: (b, g, 0, 0),
                               pipeline_mode=pl.Buffered(1)),
                  pl.BlockSpec((None, n_blocks, width), lambda b, g, i: (b, 0, g))],
        out_specs=pl.BlockSpec((None, tq, width), lambda b, g, i: (b, i, g)),
        out_shape=jax.ShapeDtypeStruct((batch, seq, D_ATT), BF16),
        scratch_shapes=[pltpu.VMEM((HEADS_PER_STEP, tq, 2 * HEAD_DIM), BF16),
                        pltpu.VMEM((HEADS_PER_STEP, tq, 2 * HEAD_DIM), BF16),
                        pltpu.VMEM((HEADS_PER_STEP, tq, 2 * tq), F32),
                        pltpu.VMEM((HEADS_PER_STEP, tq, 2 * tq), BF16),
                        pltpu.VMEM((HEADS_PER_STEP, tq, LANES), F32),
                        pltpu.VMEM((HEADS_PER_STEP, tq, LANES), F32)],
        compiler_params=pltpu.CompilerParams(dimension_semantics=("arbitrary",) * 3, vmem_limit_bytes=VMEM_LIMIT),
        name="moba_prompt",
    )(slopes, q, ka, va, kmean)


def _moba_sample_kernel(n_past_blocks, past_len, pt_ref, q_ref, kn_ref, vn_ref, slope_ref, tpos_ref, hmask_ref,
                        *rest):
    k_pages = rest[:PAGES_PER_STEP]
    v_pages = rest[PAGES_PER_STEP:2 * PAGES_PER_STEP]
    o_ref, o_all, m_all, l_all, g_all, s_ref, p_ref = rest[2 * PAGES_PER_STEP:]
    del pt_ref
    j = pl.program_id(1)
    n_tok = q_ref.shape[0]
    n_rows = n_tok * N_HEADS
    hmask = hmask_ref[...]
    slope = slope_ref[...]
    tpos = tpos_ref[...]
    q = q_ref[...]
    qbd = jnp.concatenate([jnp.broadcast_to(q[t:t + 1, :], (N_HEADS, D_ATT)) for t in range(n_tok)], axis=0) * hmask
    q_hilo = jnp.concatenate(_hilo(qbd), axis=0)
    colk = lax.broadcasted_iota(jnp.int32, (n_rows, MOBA_BLOCK), 1).astype(F32)
    colb = lax.broadcasted_iota(jnp.int32, (n_rows, LANES), 1)

    @pl.when(j == 0)
    def _():
        m_all[...] = jnp.full(m_all.shape, -jnp.inf, F32)
        l_all[...] = jnp.zeros(l_all.shape, F32)
        g_all[...] = jnp.full(g_all.shape, -jnp.inf, F32)

    def block_t(page_refs, i):
        pages = range(i * PAGES_PER_BLOCK, (i + 1) * PAGES_PER_BLOCK)
        return jnp.concatenate([page_refs[p][...] for p in pages], axis=1).astype(BF16)

    def scores(i):
        kt = block_t(k_pages, i)
        both = jnp.dot(q_hilo, kt, preferred_element_type=F32)
        s_ref[i] = both[:n_rows] + both[n_rows:]

    def softmax(i):
        n = j * BLOCKS_PER_STEP + i
        raw = s_ref[i]
        gate = jnp.mean(raw, axis=-1, keepdims=True)
        dist = (past_len + tpos - (n * MOBA_BLOCK).astype(F32)) - colk
        s = raw - slope * dist
        m = jnp.max(s, axis=-1, keepdims=True)
        p = jnp.exp(s - m)
        l = jnp.sum(p, axis=-1, keepdims=True)
        p_ref[i] = p.astype(BF16)
        here = colb == n
        m_all[...] = jnp.where(here, m, m_all[...])
        l_all[...] = jnp.where(here, l, l_all[...])
        g_all[...] = jnp.where(here, gate, g_all[...])

    def values(i):
        o_all[j * BLOCKS_PER_STEP + i] = _dot_nt(p_ref[i], block_t(v_pages, i))

    scores(0)
    for i in range(BLOCKS_PER_STEP):
        if i + 1 < BLOCKS_PER_STEP:
            scores(i + 1)
        softmax(i)
        values(i)

    @pl.when(j == pl.num_programs(1) - 1)
    def _():
        sel = _top_blocks(g_all[...], colb, min(MOBA_TOPK, n_past_blocks))
        kn = kn_ref[...]
        s_own = []
        for t in range(n_tok):
            s_t = jnp.sum(qbd * kn[t:t + 1, :], axis=-1, keepdims=True) - slope * (tpos - float(t))
            s_own.append(jnp.where(tpos >= float(t), s_t, -jnp.inf))
        m_own = functools.reduce(jnp.maximum, s_own)
        m_sel = jnp.max(jnp.where(sel, m_all[...], -jnp.inf), axis=-1, keepdims=True)
        m_tot = jnp.maximum(m_own, m_sel)
        w = jnp.where(sel, jnp.exp(m_all[...] - m_tot), 0.0)
        l_tot = jnp.sum(w * l_all[...], axis=-1, keepdims=True)
        vn = vn_ref[...]
        acc = jnp.zeros((n_rows, D_ATT), F32)
        for t in range(n_tok):
            p_t = jnp.exp(s_own[t] - m_tot)
            l_tot = l_tot + p_t
            acc = acc + p_t * vn[t:t + 1, :]
        for n in range(n_past_blocks):
            acc = acc + w[:, n:n + 1] * o_all[n]
        out = acc * hmask / l_tot
        o_ref[...] = jnp.sum(out.reshape(n_tok, N_HEADS, D_ATT), axis=1)


def _moba_sample(layer, page_table, q, k_new, v_new, cache_kt, cache_vt, slope_rows, tpos_rows, hmask, past_len):
    dec_batch, n_tok, _ = q.shape
    n_pages = page_table.shape[1]
    assert past_len % MOBA_BLOCK == 0 and n_pages * PAGE_SIZE == past_len and n_pages % PAGES_PER_STEP == 0
    n_past_blocks = past_len // MOBA_BLOCK
    n_rows = n_tok * N_HEADS
    assert 0 < n_past_blocks <= LANES
    assert n_rows % 16 == 0
    n_steps = n_pages // PAGES_PER_STEP

    def tokspec():
        return pl.BlockSpec((None, n_tok, D_ATT), lambda b, j, pt: (b, 0, 0))

    def const(shape):
        return pl.BlockSpec(shape, lambda b, j, pt: (0,) * len(shape))

    def pagespec(i):
        return pl.BlockSpec((None, None, D_ATT, PAGE_SIZE),
                            lambda b, j, pt: (layer, pt[b * n_pages + j * PAGES_PER_STEP + i], 0, 0))

    grid_spec = pltpu.PrefetchScalarGridSpec(
        num_scalar_prefetch=1,
        grid=(dec_batch, n_steps),
        in_specs=[tokspec(), tokspec(), tokspec(),
                  const((n_rows, 1)), const((n_rows, 1)), const((n_rows, D_ATT))]
        + [pagespec(i) for i in range(PAGES_PER_STEP)] * 2,
        out_specs=tokspec(),
        scratch_shapes=[pltpu.VMEM((n_past_blocks, n_rows, D_ATT), F32),
                        pltpu.VMEM((n_rows, LANES), F32),
                        pltpu.VMEM((n_rows, LANES), F32),
                        pltpu.VMEM((n_rows, LANES), F32),
                        pltpu.VMEM((BLOCKS_PER_STEP, n_rows, MOBA_BLOCK), F32),
                        pltpu.VMEM((BLOCKS_PER_STEP, n_rows, MOBA_BLOCK), BF16)],
    )
    return pl.pallas_call(
        functools.partial(_moba_sample_kernel, n_past_blocks, float(past_len)),
        grid_spec=grid_spec,
        out_shape=jax.ShapeDtypeStruct((dec_batch, n_tok, D_ATT), F32),
        compiler_params=pltpu.CompilerParams(dimension_semantics=("arbitrary",) * 2, vmem_limit_bytes=VMEM_LIMIT),
        name="moba_sample",
    )(page_table.reshape(-1), q, k_new, v_new, slope_rows, tpos_rows, hmask,
      *([cache_kt] * PAGES_PER_STEP), *([cache_vt] * PAGES_PER_STEP))


def _rglru_coeffs(xc, wa_ref, ba_ref, wx_ref, bx_ref, lam_ref):
    xb = xc.astype(BF16)

    def block_diag(w_ref):
        return jnp.concatenate(
            [jnp.dot(xb[:, n * RNN_BLOCK:(n + 1) * RNN_BLOCK], w_ref[n], preferred_element_type=F32)
             for n in range(N_RNN_BLOCKS)], axis=-1)

    r = jax.nn.sigmoid(block_diag(wa_ref) + ba_ref[...])
    gate_i = jax.nn.sigmoid(block_diag(wx_ref) + bx_ref[...])
    z = -lam_ref[...]
    softplus = jnp.maximum(z, 0.0) + jnp.log1p(jnp.exp(-jnp.abs(z)))
    log_a = -LRU_C * r * softplus
    a = jnp.exp(log_a)
    t = 1.0 - a * a
    b = jnp.where(t > 0.0, t * lax.rsqrt(t), 0.0) * (gate_i * xc)
    return a, b


def _rglru_prompt_kernel(x_ref, cw_ref, cb_ref, wa_ref, ba_ref, wx_ref, bx_ref, lam_ref,
                         y_ref, conv_ref, hlast_ref, xtail, hcarry):
    t = pl.program_id(1)
    tt = x_ref.shape[0]

    @pl.when(t == 0)
    def _():
        xtail[...] = jnp.zeros(xtail.shape, F32)
        hcarry[...] = jnp.zeros(hcarry.shape, F32)

    n_grp = tt // SCAN_GROUP
    x = x_ref[...].reshape(n_grp, SCAN_GROUP, D_RNN)
    tail = xtail[...]
    row = lax.broadcasted_iota(jnp.int32, (n_grp, SCAN_GROUP, D_RNN), 1)
    xc = cb_ref[...] + x * cw_ref[CONV_W - 1:CONV_W, :]
    for d in range(1, CONV_W):
        rot = pltpu.roll(x, d, 1)
        before = jnp.concatenate([pltpu.roll(tail, d, 0)[None], rot[:-1]], axis=0)
        xs = jnp.where(row < d, before, rot)
        xc = xc + xs * cw_ref[CONV_W - 1 - d:CONV_W - d, :]
    a, b = _rglru_coeffs(xc.reshape(tt, D_RNN), wa_ref, ba_ref, wx_ref, bx_ref, lam_ref)
    a = a.reshape(n_grp, SCAN_GROUP, D_RNN)
    b = b.reshape(n_grp, SCAN_GROUP, D_RNN)
    d = 1
    while d < SCAN_GROUP:
        keep = row >= d
        b = jnp.where(keep, a * pltpu.roll(b, d, 1) + b, b)
        a = jnp.where(keep, a * pltpu.roll(a, d, 1), a)
        d *= 2
    carry = hcarry[...]
    for g2 in range(n_grp // 2):
        h_pair = []
        for g in (2 * g2, 2 * g2 + 1):
            h_pair.append(a[g] * carry + b[g])
            carry = h_pair[-1][SCAN_GROUP - 1:SCAN_GROUP]
        y_ref[2 * g2 * SCAN_GROUP:2 * (g2 + 1) * SCAN_GROUP, :] = jnp.concatenate(h_pair, axis=0).astype(y_ref.dtype)
    xtail[...] = x[n_grp - 1]
    hcarry[...] = carry

    @pl.when(t == pl.num_programs(1) - 1)
    def _():
        conv_ref[...] = x_ref[tt - (CONV_W - 1):tt, :]
        hlast_ref[...] = carry


def _rglru_prompt(xr, conv_w, conv_b, wa_b, b_a, wx_b, b_x, lam):
    batch, seq, _ = xr.shape
    tt = 512
    assert seq % tt == 0 and tt % (2 * SCAN_GROUP) == 0

    def const(shape):
        return pl.BlockSpec(shape, lambda b, t: (0,) * len(shape))

    wspec = const((N_RNN_BLOCKS, RNN_BLOCK, RNN_BLOCK))
    vec = const((1, D_RNN))
    return pl.pallas_call(
        _rglru_prompt_kernel,
        grid=(batch, seq // tt),
        in_specs=[pl.BlockSpec((None, tt, D_RNN), lambda b, t: (b, t, 0)),
                  const((CONV_W, D_RNN)), vec, wspec, vec, wspec, vec, vec],
        out_specs=[pl.BlockSpec((None, tt, D_RNN), lambda b, t: (b, t, 0)),
                   pl.BlockSpec((None, CONV_W - 1, D_RNN), lambda b, t: (b, 0, 0)),
                   pl.BlockSpec((None, 1, D_RNN), lambda b, t: (b, 0, 0))],
        out_shape=[jax.ShapeDtypeStruct((batch, seq, D_RNN), BF16),
                   jax.ShapeDtypeStruct((batch, CONV_W - 1, D_RNN), F32),
                   jax.ShapeDtypeStruct((batch, 1, D_RNN), F32)],
        scratch_shapes=[pltpu.VMEM((8, D_RNN), F32), pltpu.VMEM((1, D_RNN), F32)],
        compiler_params=pltpu.CompilerParams(dimension_semantics=("arbitrary",) * 2, vmem_limit_bytes=VMEM_LIMIT),
        name="rglru_prompt",
    )(xr, conv_w, conv_b, wa_b, b_a, wx_b, b_x, lam)


def _rglru_sample_kernel(x_ref, prev_ref, h0_ref, cw_ref, cb_ref, wa_ref, ba_ref, wx_ref, bx_ref, lam_ref,
                         y_ref, hlast_ref):
    n_tok, n_seq = x_ref.shape[0], x_ref.shape[1]
    xp = [prev_ref[i] for i in range(CONV_W - 1)] + [x_ref[i] for i in range(n_tok)]
    xc = [cb_ref[...] + functools.reduce(lambda u, w: u + w, [xp[t + j] * cw_ref[j:j + 1, :] for j in range(CONV_W)])
          for t in range(n_tok)]
    a, b = _rglru_coeffs(jnp.concatenate(xc, axis=0), wa_ref, ba_ref, wx_ref, bx_ref, lam_ref)
    h = h0_ref[...]
    for t in range(n_tok):
        h = a[t * n_seq:(t + 1) * n_seq] * h + b[t * n_seq:(t + 1) * n_seq]
        y_ref[t] = h
    hlast_ref[...] = h


def _rglru_sample(xr_t, prev_t, h0, conv_w, conv_b, wa_b, b_a, wx_b, b_x, lam):
    n_tok, n_seq, _ = xr_t.shape
    assert n_seq % 8 == 0
    return pl.pallas_call(
        _rglru_sample_kernel,
        out_shape=[jax.ShapeDtypeStruct((n_tok, n_seq, D_RNN), F32), jax.ShapeDtypeStruct((n_seq, D_RNN), F32)],
        compiler_params=pltpu.CompilerParams(vmem_limit_bytes=VMEM_LIMIT),
        name="rglru_sample",
    )(xr_t, prev_t, h0, conv_w, conv_b, wa_b, b_a, wx_b, b_x, lam)


def _merge_kernel(x_ref, att_ref, ga_ref, rnn_ref, gr_ref, ma_ref, mr_ref, wao_ref, wro_ref, wo_ref, y_ref):
    def f32(ref):
        return ref[...].astype(F32)

    y_a = jnp.dot((f32(att_ref) * jax.nn.silu(f32(ga_ref))).astype(BF16), wao_ref[...], preferred_element_type=F32)
    y_r = jnp.dot((f32(rnn_ref) * jax.nn.silu(f32(gr_ref))).astype(BF16), wro_ref[...], preferred_element_type=F32)
    z = jax.nn.sigmoid(f32(ma_ref)) * y_a + jax.nn.sigmoid(f32(mr_ref)) * y_r
    y_ref[...] = x_ref[...] + jnp.dot(z.astype(BF16), wo_ref[...], preferred_element_type=F32)


def _merge(x, att, ga, rnn, gr, ma, mr, wao_b, wro_b, wo_b):
    rows = x.shape[0]
    tm = min(rows, 512)
    assert rows % tm == 0

    def rowspec(width):
        return pl.BlockSpec((tm, width), lambda i: (i, 0))

    def const(shape):
        return pl.BlockSpec(shape, lambda i: (0,) * len(shape))

    return pl.pallas_call(
        _merge_kernel,
        grid=(rows // tm,),
        in_specs=[rowspec(D_MODEL), rowspec(D_ATT), rowspec(D_ATT), rowspec(D_RNN), rowspec(D_RNN),
                  rowspec(D_MODEL), rowspec(D_MODEL),
                  const((D_ATT, D_MODEL)), const((D_RNN, D_MODEL)), const((D_MODEL, D_MODEL))],
        out_specs=rowspec(D_MODEL),
        out_shape=jax.ShapeDtypeStruct((rows, D_MODEL), F32),
        compiler_params=pltpu.CompilerParams(dimension_semantics=("arbitrary",), vmem_limit_bytes=VMEM_LIMIT),
        name="merge",
    )(x, att, ga, rnn, gr, ma, mr, wao_b, wro_b, wo_b)


def kernel(x_prompt, x_sample, cache_k, cache_v, state_conv, state_h, page_table, norm_w, w_in, q_norm_w, k_norm_w,
           conv_w, conv_b, w_gate_a, b_gate_a, w_gate_x, b_gate_x, lru_lambda, w_attn_out, w_rnn_out, w_out):
    batch, seq, _ = x_prompt.shape
    dec_batch, dec_seq, _ = x_sample.shape
    depth = w_in.shape[0]
    n_pool = cache_k.shape[1]
    past_len = page_table.shape[1] * PAGE_SIZE
    n_blocks = seq // MOBA_BLOCK

    slopes = jnp.asarray([2.0 ** (-8.0 * (h + 1) / N_HEADS) for h in range(N_HEADS)], F32)
    head_of_lane = np.arange(D_ATT) // HEAD_DIM
    gmean = jnp.asarray((head_of_lane[:, None] == head_of_lane[None, :]) / HEAD_DIM, BF16)
    row_head = np.arange(dec_seq * N_HEADS) % N_HEADS
    slope_rows = jnp.asarray(2.0 ** (-8.0 * (row_head + 1) / N_HEADS), F32)[:, None]
    tpos_rows = jnp.asarray(np.arange(dec_seq * N_HEADS) // N_HEADS, F32)[:, None]
    hmask = jnp.asarray(row_head[:, None] == head_of_lane[None, :], F32)

    w_in_b = w_in.astype(BF16)
    wa_b = w_gate_a.astype(BF16)
    wx_b = w_gate_x.astype(BF16)
    wao_b = w_attn_out.astype(BF16)
    wro_b = w_rnn_out.astype(BF16)
    wo_b = w_out.astype(BF16)
    cache_kt = jnp.transpose(cache_k, (0, 1, 3, 4, 2)).reshape(depth, n_pool, D_ATT, PAGE_SIZE)
    cache_vt = jnp.transpose(cache_v, (0, 1, 3, 4, 2)).reshape(depth, n_pool, D_ATT, PAGE_SIZE)

    y_p = x_prompt.reshape(batch * seq, D_MODEL)
    y_s = x_sample.reshape(dec_batch * dec_seq, D_MODEL)
    outs = {name: [] for name in ("cp", "hp", "ks", "vs", "cs", "hs")}
    kt_all = vt_all = None
    for l in range(depth):
        nw = norm_w[l][None, :]
        qn_t = jnp.tile(q_norm_w[l], N_HEADS)[None, :]
        kn_t = jnp.tile(k_norm_w[l], N_HEADS)[None, :]
        rnn_w = (conv_w[l], conv_b[l][None, :], wa_b[l], b_gate_a[l][None, :], wx_b[l], b_gate_x[l][None, :],
                 lru_lambda[l][None, :])
        out_w = (wao_b[l], wro_b[l], wo_b[l])

        q, kt_all, vt_all, ka, va, kmean, ga, xr, gr, ma, mr = _proj_prompt(
            y_p, batch, l, depth, None if l == 0 else (kt_all, vt_all), nw, w_in_b[l], qn_t, kn_t, gmean)
        att = _moba_prompt(slopes, q.reshape(batch, seq, D_ATT),
                           ka.reshape(batch, n_blocks, N_HEADS, 2 * HEAD_DIM, MOBA_BLOCK), va,
                           kmean.reshape(batch, n_blocks, D_ATT))
        rnn, c_new, h_new = _rglru_prompt(xr.reshape(batch, seq, D_RNN), *rnn_w)
        y_p = _merge(y_p, att.reshape(batch * seq, D_ATT), ga, rnn.reshape(batch * seq, D_RNN), gr, ma, mr, *out_w)
        outs["cp"].append(c_new)
        outs["hp"].append(h_new.reshape(batch, D_RNN))

        q, k, v, ga, xr, gr, ma, mr = _proj_sample(y_s, nw, w_in_b[l], qn_t, kn_t, gmean)
        att = _moba_sample(l, page_table, q.reshape(dec_batch, dec_seq, D_ATT), k.reshape(dec_batch, dec_seq, D_ATT),
                           v.reshape(dec_batch, dec_seq, D_ATT), cache_kt, cache_vt,
                           slope_rows, tpos_rows, hmask, past_len)
        xr_t = jnp.swapaxes(xr.reshape(dec_batch, dec_seq, D_RNN), 0, 1)
        prev_t = jnp.swapaxes(state_conv[l], 0, 1)
        rnn_t, h_new = _rglru_sample(xr_t, prev_t, state_h[l], *rnn_w)
        rnn = jnp.swapaxes(rnn_t, 0, 1).reshape(dec_batch * dec_seq, D_RNN)
        c_new = jnp.swapaxes(jnp.concatenate([prev_t, xr_t], axis=0)[dec_seq:], 0, 1)
        y_s = _merge(y_s, att.reshape(dec_batch * dec_seq, D_ATT), ga, rnn, gr, ma, mr, *out_w)
        outs["ks"].append(k.reshape(dec_batch, dec_seq, N_HEADS, HEAD_DIM))
        outs["vs"].append(v.reshape(dec_batch, dec_seq, N_HEADS, HEAD_DIM))
        outs["cs"].append(c_new)
        outs["hs"].append(h_new)

    def from_transposed(t):
        return jnp.transpose(t.reshape(depth, batch, N_HEADS, HEAD_DIM, seq), (0, 1, 4, 2, 3))

    return (y_p.reshape(batch, seq, D_MODEL), y_s.reshape(dec_batch, dec_seq, D_MODEL),
            from_transposed(kt_all), from_transposed(vt_all), jnp.stack(outs["cp"]), jnp.stack(outs["hp"]),
            jnp.stack(outs["ks"]), jnp.stack(outs["vs"]), jnp.stack(outs["cs"]), jnp.stack(outs["hs"]))
```
